```python
import math
import jax, jax.numpy as jnp
from jax import lax
import numpy as np

D_MODEL = 1024
BATCH = 4
SEQ = 8192
DEPTH = 2

CONV_WIDTH = 31
HEAD_DIM = 64
N_Q_HEADS = D_MODEL // HEAD_DIM
N_KV_HEADS = 4
GROUP = N_Q_HEADS // N_KV_HEADS
WINDOW = 128
BLOCK = 128
NUM_BUCKETS = 32
MAX_DISTANCE = 128
D_FF = ((8 * D_MODEL // 3 + 255) // 256) * 256
EPS = 1e-6

kernel_name = "hybrid_conv_swa_sink_t5_swiglu"


def rms_norm(x, g):
    xf = x.astype(jnp.float32)
    y = xf * lax.rsqrt(jnp.mean(xf * xf, axis=-1, keepdims=True) + EPS)
    return (y * g.astype(jnp.float32)).astype(x.dtype)


def layer_norm(x, g, b):
    xf = x.astype(jnp.float32)
    mu = jnp.mean(xf, axis=-1, keepdims=True)
    xc = xf - mu
    var = jnp.mean(xc * xc, axis=-1, keepdims=True)
    y = xc * lax.rsqrt(var + EPS) * g.astype(jnp.float32) + b.astype(jnp.float32)
    return y.astype(x.dtype)


def conformer_conv(x, w_in, b_in, dw_w, dw_b, ln_g, ln_b, w_out, b_out):
    u = x @ w_in + b_in
    val, gate = jnp.split(u, 2, axis=-1)
    u = val * jax.nn.sigmoid(gate)
    u = lax.conv_general_dilated(
        u, dw_w[:, None, :].astype(u.dtype), window_strides=(1,),
        padding=[(CONV_WIDTH - 1, 0)],
        dimension_numbers=("NWC", "WIO", "NWC"),
        feature_group_count=D_MODEL) + dw_b
    u = jax.nn.silu(layer_norm(u, ln_g, ln_b))
    return u @ w_out + b_out


def t5_causal_bucket(dist):
    dist = jnp.maximum(dist, 0)
    max_exact = NUM_BUCKETS // 2
    large = max_exact + (
        jnp.log(jnp.maximum(dist, 1).astype(jnp.float32) / max_exact)
        / math.log(MAX_DISTANCE / max_exact) * (NUM_BUCKETS - max_exact)
    ).astype(jnp.int32)
    large = jnp.minimum(large, NUM_BUCKETS - 1)
    return jnp.where(dist < max_exact, dist, large)


def swa_sink_attention(x, w_qkv, b_qkv, w_o, b_o, sinks, rel_bias):
    B, S, _ = x.shape
    nb = S // BLOCK
    qkv = x @ w_qkv + b_qkv
    q, k, v = jnp.split(qkv, [N_Q_HEADS * HEAD_DIM, (N_Q_HEADS + N_KV_HEADS) * HEAD_DIM], axis=-1)
    q = q.reshape(B, nb, BLOCK, N_KV_HEADS, GROUP, HEAD_DIM)
    k = k.reshape(B, nb, BLOCK, N_KV_HEADS, HEAD_DIM)
    v = v.reshape(B, nb, BLOCK, N_KV_HEADS, HEAD_DIM)
    k_prev = jnp.concatenate([jnp.zeros_like(k[:, :1]), k[:, :-1]], axis=1)
    v_prev = jnp.concatenate([jnp.zeros_like(v[:, :1]), v[:, :-1]], axis=1)
    kb = jnp.concatenate([k_prev, k], axis=2)
    vb = jnp.concatenate([v_prev, v], axis=2)

    scale = HEAD_DIM ** -0.5
    logits = jnp.einsum("bnqkgd,bnskd->bnkgqs", q, kb,
                        preferred_element_type=jnp.float32) * scale

    q_loc = jnp.arange(BLOCK, dtype=jnp.int32)[:, None] + BLOCK
    s_loc = jnp.arange(2 * BLOCK, dtype=jnp.int32)[None, :]
    dist = q_loc - s_loc
    band = (dist >= 0) & (dist < WINDOW)
    bias = rel_bias.astype(jnp.float32)[t5_causal_bucket(dist)]
    bias = jnp.transpose(bias, (2, 0, 1)).reshape(N_KV_HEADS, GROUP, BLOCK, 2 * BLOCK)
    blk = jnp.arange(nb, dtype=jnp.int32)[:, None, None]
    valid = band[None] & ((blk * BLOCK - BLOCK + s_loc[None]) >= 0)

    logits = jnp.where(valid[None, :, None, None], logits + bias, -jnp.inf)
    sink = sinks.astype(jnp.float32).reshape(N_KV_HEADS, GROUP)[None, None, :, :, None, None]
    m = jnp.maximum(jnp.max(logits, axis=-1, keepdims=True), sink)
    p = jnp.exp(logits - m)
    denom = jnp.sum(p, axis=-1, keepdims=True) + jnp.exp(sink - m)
    p = (p / denom).astype(vb.dtype)
    out = jnp.einsum("bnkgqs,bnskd->bnqkgd", p, vb).reshape(B, S, N_Q_HEADS * HEAD_DIM)
    return out @ w_o + b_o


def swiglu_ffn(x, w_gate_up, w_down):
    g, u = jnp.split(x @ w_gate_up, 2, axis=-1)
    return (jax.nn.silu(g) * u) @ w_down


def setup_inputs(seed: int = 0) -> dict:
    key = jax.random.key(seed)
    keys = iter(jax.random.split(key, 64))
    n_conv = (DEPTH + 1) // 2
    n_attn = DEPTH // 2
    f32 = jnp.float32

    def w(shape, fan_in):
        return jax.random.normal(next(keys), shape, f32) * fan_in ** -0.5

    def gain(shape):
        return 1.0 + 0.05 * jax.random.normal(next(keys), shape, f32)

    def small(shape, s=0.02):
        return s * jax.random.normal(next(keys), shape, f32)

    qkv_w = (N_Q_HEADS + 2 * N_KV_HEADS) * HEAD_DIM
    return {
        "x": jax.random.normal(next(keys), (BATCH, SEQ, D_MODEL), f32),
        "mix_pre_g": gain((DEPTH, D_MODEL)),
        "mix_post_g": gain((DEPTH, D_MODEL)),
        "ffn_pre_g": gain((DEPTH, D_MODEL)),
        "ffn_post_g": gain((DEPTH, D_MODEL)),
        "conv_w_in": w((n_conv, D_MODEL, 2 * D_MODEL), D_MODEL),
        "conv_b_in": small((n_conv, 2 * D_MODEL)),
        "conv_dw_w": w((n_conv, CONV_WIDTH, D_MODEL), CONV_WIDTH),
        "conv_dw_b": small((n_conv, D_MODEL)),
        "conv_ln_g": gain((n_conv, D_MODEL)),
        "conv_ln_b": small((n_conv, D_MODEL)),
        "conv_w_out": w((n_conv, D_MODEL, D_MODEL), D_MODEL),
        "conv_b_out": small((n_conv, D_MODEL)),
        "attn_w_qkv": w((n_attn, D_MODEL, qkv_w), D_MODEL),
        "attn_b_qkv": small((n_attn, qkv_w)),
        "attn_w_o": w((n_attn, N_Q_HEADS * HEAD_DIM, D_MODEL), N_Q_HEADS * HEAD_DIM),
        "attn_b_o": small((n_attn, D_MODEL)),
        "attn_sinks": 0.5 * jax.random.normal(next(keys), (n_attn, N_Q_HEADS), f32),
        "rel_bias": 0.1 * jax.random.normal(next(keys), (NUM_BUCKETS, N_Q_HEADS), f32),
        "ffn_w_gate_up": w((DEPTH, D_MODEL, 2 * D_FF), D_MODEL),
        "ffn_w_down": w((DEPTH, D_FF, D_MODEL), D_FF),
    }


def reference(x, mix_pre_g, mix_post_g, ffn_pre_g, ffn_post_g,
              conv_w_in, conv_b_in, conv_dw_w, conv_dw_b, conv_ln_g, conv_ln_b,
              conv_w_out, conv_b_out,
              attn_w_qkv, attn_b_qkv, attn_w_o, attn_b_o, attn_sinks, rel_bias,
              ffn_w_gate_up, ffn_w_down):
    h = x
    for i in range(DEPTH):
        j = i // 2
        u = rms_norm(h, mix_pre_g[i])
        if i % 2 == 0:
            u = conformer_conv(u, conv_w_in[j], conv_b_in[j], conv_dw_w[j], conv_dw_b[j],
                               conv_ln_g[j], conv_ln_b[j], conv_w_out[j], conv_b_out[j])
        else:
            u = swa_sink_attention(u, attn_w_qkv[j], attn_b_qkv[j], attn_w_o[j], attn_b_o[j],
                                   attn_sinks[j], rel_bias)
        h = h + rms_norm(u, mix_post_g[i])
        f = swiglu_ffn(rms_norm(h, ffn_pre_g[i]), ffn_w_gate_up[i], ffn_w_down[i])
        h = h + rms_norm(f, ffn_post_g[i])
    return h
```

```python
import functools
import math

import jax
import jax.numpy as jnp
import numpy as np
from jax import lax
from jax.experimental import pallas as pl
from jax.experimental.pallas import tpu as pltpu

D_MODEL = 1024
CONV_WIDTH = 31
HEAD_DIM = 64
N_Q_HEADS = D_MODEL // HEAD_DIM
N_KV_HEADS = 4
GROUP = N_Q_HEADS // N_KV_HEADS
WINDOW = 128
BLOCK = 128
NUM_BUCKETS = 32
MAX_DISTANCE = 128
D_FF = ((8 * D_MODEL // 3 + 255) // 256) * 256
EPS = 1e-6

V7X_LANES = 128
V7X_SUBLANES = 8
V7X_VMEM_BYTES = 64 * 1024 * 1024

TOKEN_TILE = 512
CONV_HALO = 32
CONV_SUBTILE = 64
VMEM_LIMIT = 56 * 1024 * 1024

F32 = jnp.float32
BF16 = jnp.bfloat16


def _rms(x, g):
    ms = jnp.mean(x * x, axis=-1, keepdims=True)
    return x * lax.rsqrt(ms + EPS) * g


def _sigmoid(x):
    return 1.0 / (1.0 + jnp.exp(-x))


def _const_spec(shape):
    zeros = (0,) * len(shape)
    return pl.BlockSpec(shape, lambda i: zeros, pipeline_mode=pl.Buffered(1))


def _tile_spec(tm, d):
    return pl.BlockSpec((tm, d), lambda i: (i, 0))


def _params():
    return pltpu.CompilerParams(
        dimension_semantics=("arbitrary",), vmem_limit_bytes=VMEM_LIMIT)


def _ffn_kernel(h_ref, pre_ref, post_ref, wgu_ref, wd_ref, o_ref):
    h = h_ref[...]
    xn = _rms(h, pre_ref[...]).astype(BF16)
    g = jnp.dot(xn, wgu_ref[:, :D_FF], preferred_element_type=F32)
    u = jnp.dot(xn, wgu_ref[:, D_FF:], preferred_element_type=F32)
    a = (g * _sigmoid(g) * u).astype(BF16)
    f = jnp.dot(a, wd_ref[...], preferred_element_type=F32)
    o_ref[...] = h + _rms(f, post_ref[...])


def _ffn_call(h, pre_g, post_g, w_gu, w_d):
    t, d = h.shape
    tm = TOKEN_TILE
    return pl.pallas_call(
        _ffn_kernel,
        grid=(t // tm,),
        in_specs=[
            _tile_spec(tm, d),
            _const_spec((1, d)),
            _const_spec((1, d)),
            _const_spec(w_gu.shape),
            _const_spec(w_d.shape),
        ],
        out_specs=_tile_spec(tm, d),
        out_shape=jax.ShapeDtypeStruct((t, d), F32),
        compiler_params=_params(),
        name="ffn",
    )(h, pre_g, post_g, w_gu, w_d)


def _conv_kernel(tiles_per_seq, h_ref, pre_ref, post_ref, win_ref, bin_ref,
                 dww_ref, dwb_ref, lng_ref, lnb_ref, wout_ref, bout_ref,
                 o_ref, ext_ref, y_ref):
    tm, d = h_ref.shape
    i = pl.program_id(0)

    @pl.when(i % tiles_per_seq == 0)
    def _():
        ext_ref[0:CONV_HALO, :] = jnp.zeros((CONV_HALO, d), F32)

    h = h_ref[...]
    xn = _rms(h, pre_ref[...]).astype(BF16)
    val = jnp.dot(xn, win_ref[:, :d], preferred_element_type=F32) + bin_ref[:, :d]
    gate = jnp.dot(xn, win_ref[:, d:], preferred_element_type=F32) + bin_ref[:, d:]
    ext_ref[CONV_HALO:CONV_HALO + tm, :] = val * _sigmoid(gate)

    off0 = CONV_HALO - (CONV_WIDTH - 1)
    ts = CONV_SUBTILE
    n_sub = tm // ts
    n_lane = d // V7X_LANES

    def body(it, carry):
        tt = it // n_lane
        cc = it % n_lane
        r0 = pl.multiple_of(tt * ts, ts)
        c0 = pl.multiple_of(cc * V7X_LANES, V7X_LANES)
        rows = ext_ref[pl.ds(r0, ts + CONV_HALO), pl.ds(c0, V7X_LANES)]
        acc = jnp.broadcast_to(dwb_ref[:, pl.ds(c0, V7X_LANES)], (ts, V7X_LANES))
        for r in range(V7X_SUBLANES):
            offs = [o for o in range(off0, off0 + CONV_WIDTH) if o % V7X_SUBLANES == r]
            hi = max(offs) - r + ts
            xr = rows[r:r + hi, :]
            for o in offs:
                a = o - r
                w = dww_ref[pl.ds(o - off0, 1), pl.ds(c0, V7X_LANES)]
                acc = acc + xr[a:a + ts, :] * w
        y_ref[pl.ds(r0, ts), pl.ds(c0, V7X_LANES)] = acc
        return carry

    lax.fori_loop(0, n_sub * n_lane, body, 0)

    ext_ref[0:CONV_HALO, :] = ext_ref[tm:tm + CONV_HALO, :]

    y = y_ref[...]
    mu = jnp.mean(y, axis=-1, keepdims=True)
    yc = y - mu
    var = jnp.mean(yc * yc, axis=-1, keepdims=True)
    z = yc * lax.rsqrt(var + EPS) * lng_ref[...] + lnb_ref[...]
    z = (z * _sigmoid(z)).astype(BF16)
    f = jnp.dot(z, wout_ref[...], preferred_element_type=F32) + bout_ref[...]
    o_ref[...] = h + _rms(f, post_ref[...])


def _conv_call(h, seq_len, pre_g, post_g, w_in, b_in, dw_w, dw_b, ln_g, ln_b,
               w_out, b_out):
    t, d = h.shape
    tm = TOKEN_TILE
    assert seq_len % tm == 0 and CONV_HALO >= CONV_WIDTH - 1
    return pl.pallas_call(
        functools.partial(_conv_kernel, seq_len // tm),
        grid=(t // tm,),
        in_specs=[
            _tile_spec(tm, d),
            _const_spec((1, d)),
            _const_spec((1, d)),
            _const_spec(w_in.shape),
            _const_spec((1, 2 * d)),
            _const_spec(dw_w.shape),
            _const_spec((1, d)),
            _const_spec((1, d)),
            _const_spec((1, d)),
            _const_spec(w_out.shape),
            _const_spec((1, d)),
        ],
        out_specs=_tile_spec(tm, d),
        out_shape=jax.ShapeDtypeStruct((t, d), F32),
        scratch_shapes=[
            pltpu.VMEM((tm + CONV_HALO, d), F32),
            pltpu.VMEM((tm, d), F32),
        ],
        compiler_params=_params(),
        name="conv_mixer",
    )(h, pre_g, post_g, w_in, b_in, dw_w, dw_b, ln_g, ln_b, w_out, b_out)


def _attn_kernel(tiles_per_seq, h_ref, pre_ref, post_ref, wqkv_ref, bqkv_ref,
                 wo_ref, bo_ref, bias_ref, mask_ref, sink_ref,
                 o_ref, q_ref, kx_ref, vx_ref, ao_ref):
    tm, d = h_ref.shape
    nblk = tm // BLOCK
    half = V7X_LANES // 2
    i = pl.program_id(0)
    seq_start = i % tiles_per_seq == 0

    @pl.when(seq_start)
    def _():
        z = jnp.zeros((N_KV_HEADS, 2, BLOCK, V7X_LANES), BF16)
        kx_ref[:, :, 0:BLOCK, :] = z
        vx_ref[:, :, 0:BLOCK, :] = z

    h = h_ref[...]
    xn = _rms(h, pre_ref[...]).astype(BF16)

    scale = HEAD_DIM ** -0.5
    qkv = jnp.dot(xn, wqkv_ref[...], preferred_element_type=F32) + bqkv_ref[...]
    for kh in range(N_KV_HEADS):
        for pr in range(2):
            c0 = (2 * kh + pr) * V7X_LANES
            qp = (qkv[:, c0:c0 + V7X_LANES] * scale).astype(BF16)
            for j in range(nblk):
                q_ref[kh, j, pr * BLOCK:(pr + 1) * BLOCK, :] = qp[j * BLOCK:(j + 1) * BLOCK, :]

    lane = lax.broadcasted_iota(jnp.int32, (tm, V7X_LANES), 1)
    low = lane < half
    for which, dst in ((0, kx_ref), (1, vx_ref)):
        base = N_Q_HEADS * HEAD_DIM + which * N_KV_HEADS * HEAD_DIM
        for col in range(N_KV_HEADS // 2):
            c0 = base + col * V7X_LANES
            x = qkv[:, c0:c0 + V7X_LANES]
            xs = pltpu.roll(x, half, axis=1)
            zero = jnp.zeros_like(x)
            kh_lo, kh_hi = 2 * col, 2 * col + 1
            dst[kh_lo, 0, BLOCK:BLOCK + tm, :] = jnp.where(low, x, zero).astype(BF16)
            dst[kh_lo, 1, BLOCK:BLOCK + tm, :] = jnp.where(low, zero, xs).astype(BF16)
            dst[kh_hi, 0, BLOCK:BLOCK + tm, :] = jnp.where(low, xs, zero).astype(BF16)
            dst[kh_hi, 1, BLOCK:BLOCK + tm, :] = jnp.where(low, zero, x).astype(BF16)

    row2 = lax.broadcasted_iota(jnp.int32, (2 * BLOCK, 1), 0)
    lane2 = lax.broadcasted_iota(jnp.int32, (2 * BLOCK, V7X_LANES), 1)
    neg_inf = jnp.float32(-jnp.inf)

    def block_body(j, carry):
        r0 = pl.multiple_of(j * BLOCK, BLOCK)
        first = jnp.logical_and(seq_start, j == 0).astype(jnp.int32)
        valid = mask_ref[first] > 0.5
        for kh in range(N_KV_HEADS):
            q2 = q_ref[kh, j]
            ls = []
            acc = None
            for eo in range(2):
                kb = kx_ref[kh, eo, pl.ds(r0, 2 * BLOCK), :]
                vb = vx_ref[kh, eo, pl.ds(r0, 2 * BLOCK), :]
                s = lax.dot_general(q2, kb, (((1,), (1,)), ((), ())),
                                    preferred_element_type=F32)
                s = jnp.where(valid, s + bias_ref[kh, eo], neg_inf)
                h_top = 4 * kh + eo
                sink = jnp.where(row2 < BLOCK, sink_ref[h_top], sink_ref[h_top + 2])
                m = jnp.maximum(jnp.max(s, axis=-1, keepdims=True), sink)
                p = jnp.exp(s - m)
                ls.append(jnp.sum(p, axis=-1, keepdims=True) + jnp.exp(sink - m))
                pv = jnp.dot(p.astype(BF16), vb, preferred_element_type=F32)
                acc = pv if acc is None else acc + pv
            out = acc / jnp.where(lane2 < half, ls[0], ls[1])
            out = out.astype(BF16)
            c0 = kh * 2 * V7X_LANES
            ao_ref[pl.ds(r0, BLOCK), c0:c0 + V7X_LANES] = out[:BLOCK]
            ao_ref[pl.ds(r0, BLOCK), c0 + V7X_LANES:c0 + 2 * V7X_LANES] = out[BLOCK:]
        return carry

    lax.fori_loop(0, nblk, block_body, 0)

    kx_ref[:, :, 0:BLOCK, :] = kx_ref[:, :, tm:tm + BLOCK, :]
    vx_ref[:, :, 0:BLOCK, :] = vx_ref[:, :, tm:tm + BLOCK, :]

    f = jnp.dot(ao_ref[...], wo_ref[...], preferred_element_type=F32) + bo_ref[...]
    o_ref[...] = h + _rms(f, post_ref[...])


def _t5_causal_bucket_np(dist):
    dist = np.maximum(dist, 0)
    max_exact = NUM_BUCKETS // 2
    large = max_exact + (
        np.log(np.maximum(dist, 1).astype(np.float32) / np.float32(max_exact))
        / np.float32(math.log(MAX_DISTANCE / max_exact)) * np.float32(NUM_BUCKETS - max_exact)
    ).astype(np.int32)
    large = np.minimum(large, NUM_BUCKETS - 1)
    return np.where(dist < max_exact, dist, large)


def _attn_tables(rel_bias):
    q_loc = np.arange(BLOCK, dtype=np.int32)[:, None] + BLOCK
    s_loc = np.arange(2 * BLOCK, dtype=np.int32)[None, :]
    dist = q_loc - s_loc
    band = (dist >= 0) & (dist < WINDOW)
    bucket = _t5_causal_bucket_np(dist)
    bias = jnp.transpose(rel_bias.astype(F32)[bucket], (2, 0, 1))
    bias = bias.reshape(N_KV_HEADS, 2, 2, BLOCK, 2 * BLOCK)
    bias = jnp.transpose(bias, (0, 2, 1, 3, 4)).reshape(
        N_KV_HEADS, 2, 2 * BLOCK, 2 * BLOCK)
    first = band & (s_loc >= BLOCK)
    mask = np.stack([np.tile(band, (2, 1)), np.tile(first, (2, 1))]).astype(np.float32)
    return bias, jnp.asarray(mask)


def _attn_call(h, seq_len, pre_g, post_g, w_qkv, b_qkv, w_o, b_o, sinks, rel_bias):
    t, d = h.shape
    tm = TOKEN_TILE
    assert seq_len % tm == 0 and tm % BLOCK == 0
    nblk = tm // BLOCK
    bias, mask = _attn_tables(rel_bias)
    return pl.pallas_call(
        functools.partial(_attn_kernel, seq_len // tm),
        grid=(t // tm,),
        in_specs=[
            _tile_spec(tm, d),
            _const_spec((1, d)),
            _const_spec((1, d)),
            _const_spec(w_qkv.shape),
            _const_spec((1, w_qkv.shape[1])),
            _const_spec(w_o.shape),
            _const_spec((1, d)),
            _const_spec(bias.shape),
            _const_spec(mask.shape),
            pl.BlockSpec(memory_space=pltpu.SMEM),
        ],
        out_specs=_tile_spec(tm, d),
        out_shape=jax.ShapeDtypeStruct((t, d), F32),
        scratch_shapes=[
            pltpu.VMEM((N_KV_HEADS, nblk, 2 * BLOCK, V7X_LANES), BF16),
            pltpu.VMEM((N_KV_HEADS, 2, BLOCK + tm, V7X_LANES), BF16),
            pltpu.VMEM((N_KV_HEADS, 2, BLOCK + tm, V7X_LANES), BF16),
            pltpu.VMEM((tm, d), BF16),
        ],
        compiler_params=_params(),
        name="attn_mixer",
    )(h, pre_g, post_g, w_qkv, b_qkv, w_o, b_o, bias, mask, sinks)


def kernel(x, mix_pre_g, mix_post_g, ffn_pre_g, ffn_post_g, conv_w_in, conv_b_in, conv_dw_w, conv_dw_b, conv_ln_g, conv_ln_b, conv_w_out, conv_b_out, attn_w_qkv, attn_b_qkv, attn_w_o, attn_b_o, attn_sinks, rel_bias, ffn_w_gate_up, ffn_w_down):
    b, s, d = x.shape
    depth = mix_pre_g.shape[0]
    h = x.reshape(b * s, d)
    row = lambda v: v.reshape(1, -1)
    for i in range(depth):
        j = i // 2
        if i % 2 == 0:
            h = _conv_call(
                h, s, row(mix_pre_g[i]), row(mix_post_g[i]),
                conv_w_in[j].astype(BF16), row(conv_b_in[j]), conv_dw_w[j],
                row(conv_dw_b[j]), row(conv_ln_g[j]), row(conv_ln_b[j]),
                conv_w_out[j].astype(BF16), row(conv_b_out[j]))
        else:
            h = _attn_call(
                h, s, row(mix_pre_g[i]), row(mix_post_g[i]),
                attn_w_qkv[j].astype(BF16), row(attn_b_qkv[j]),
                attn_w_o[j].astype(BF16), row(attn_b_o[j]), attn_sinks[j], rel_bias)
        h = _ffn_call(h, row(ffn_pre_g[i]), row(ffn_post_g[i]),
                      ffn_w_gate_up[i].astype(BF16), ffn_w_down[i].astype(BF16))
    return h.reshape(b, s, d)
```

```python
import functools
import math

import jax
import jax.numpy as jnp
import numpy as np
from jax import lax
from jax.experimental import pallas as pl
from jax.experimental.pallas import tpu as pltpu

D_MODEL = 1024
CONV_WIDTH = 31
HEAD_DIM = 64
N_Q_HEADS = D_MODEL // HEAD_DIM
N_KV_HEADS = 4
GROUP = N_Q_HEADS // N_KV_HEADS
WINDOW = 128
BLOCK = 128
NUM_BUCKETS = 32
MAX_DISTANCE = 128
D_FF = ((8 * D_MODEL // 3 + 255) // 256) * 256
EPS = 1e-6

V7X_LANES = 128
V7X_SUBLANES = 8
V7X_VMEM_BYTES = 64 * 1024 * 1024

TOKEN_TILE = 512
CONV_HALO = 32
CONV_SUBTILE = 64
VMEM_LIMIT = 56 * 1024 * 1024

F32 = jnp.float32
BF16 = jnp.bfloat16


def _rms(x, g):
    ms = jnp.mean(x * x, axis=-1, keepdims=True)
    return x * lax.rsqrt(ms + EPS) * g


def _sigmoid(x):
    return 1.0 / (1.0 + jnp.exp(-x))


def _const_spec(shape):
    zeros = (0,) * len(shape)
    return pl.BlockSpec(shape, lambda i: zeros, pipeline_mode=pl.Buffered(1))


def _tile_spec(tm, d):
    return pl.BlockSpec((tm, d), lambda i: (i, 0))


def _params():
    return pltpu.CompilerParams(
        dimension_semantics=("arbitrary",), vmem_limit_bytes=VMEM_LIMIT)


def _ffn_kernel(h_ref, pre_ref, post_ref, wgu_ref, wd_ref, o_ref):
    h = h_ref[...]
    xn = _rms(h, pre_ref[...]).astype(BF16)
    g = jnp.dot(xn, wgu_ref[:, :D_FF], preferred_element_type=F32)
    u = jnp.dot(xn, wgu_ref[:, D_FF:], preferred_element_type=F32)
    a = (g * _sigmoid(g) * u).astype(BF16)
    f = jnp.dot(a, wd_ref[...], preferred_element_type=F32)
    o_ref[...] = h + _rms(f, post_ref[...])


def _ffn_call(h, pre_g, post_g, w_gu, w_d):
    t, d = h.shape
    tm = TOKEN_TILE
    return pl.pallas_call(
        _ffn_kernel,
        grid=(t // tm,),
        in_specs=[
            _tile_spec(tm, d),
            _const_spec((1, d)),
            _const_spec((1, d)),
            _const_spec(w_gu.shape),
            _const_spec(w_d.shape),
        ],
        out_specs=_tile_spec(tm, d),
        out_shape=jax.ShapeDtypeStruct((t, d), F32),
        compiler_params=_params(),
        name="ffn",
    )(h, pre_g, post_g, w_gu, w_d)


def _conv_kernel(tiles_per_seq, h_ref, pre_ref, post_ref, win_ref, bin_ref,
                 dww_ref, dwb_ref, lng_ref, lnb_ref, wout_ref, bout_ref,
                 o_ref, ext_ref, y_ref):
    tm, d = h_ref.shape
    i = pl.program_id(0)

    @pl.when(i % tiles_per_seq == 0)
    def _():
        ext_ref[0:CONV_HALO, :] = jnp.zeros((CONV_HALO, d), F32)

    h = h_ref[...]
    xn = _rms(h, pre_ref[...]).astype(BF16)
    val = jnp.dot(xn, win_ref[:, :d], preferred_element_type=F32) + bin_ref[:, :d]
    gate = jnp.dot(xn, win_ref[:, d:], preferred_element_type=F32) + bin_ref[:, d:]
    ext_ref[CONV_HALO:CONV_HALO + tm, :] = val * _sigmoid(gate)

    off0 = CONV_HALO - (CONV_WIDTH - 1)
    ts = CONV_SUBTILE
    n_sub = tm // ts
    n_lane = d // V7X_LANES

    def body(it, carry):
        tt = it // n_lane
        cc = it % n_lane
        r0 = pl.multiple_of(tt * ts, ts)
        c0 = pl.multiple_of(cc * V7X_LANES, V7X_LANES)
        rows = ext_ref[pl.ds(r0, ts + CONV_HALO), pl.ds(c0, V7X_LANES)]
        acc = jnp.broadcast_to(dwb_ref[:, pl.ds(c0, V7X_LANES)], (ts, V7X_LANES))
        n_rows = ts + CONV_HALO
        for r in range(V7X_SUBLANES):
            offs = [o for o in range(off0, off0 + CONV_WIDTH) if o % V7X_SUBLANES == r]
            xr = rows if r == 0 else pltpu.roll(rows, n_rows - r, axis=0)
            for o in offs:
                a = o - r
                w = dww_ref[pl.ds(o - off0, 1), pl.ds(c0, V7X_LANES)]
                acc = acc + xr[a:a + ts, :] * w
        y_ref[pl.ds(r0, ts), pl.ds(c0, V7X_LANES)] = acc
        return carry

    lax.fori_loop(0, n_sub * n_lane, body, 0, unroll=4)

    ext_ref[0:CONV_HALO, :] = ext_ref[tm:tm + CONV_HALO, :]

    y = y_ref[...]
    mu = jnp.mean(y, axis=-1, keepdims=True)
    yc = y - mu
    var = jnp.mean(yc * yc, axis=-1, keepdims=True)
    z = yc * lax.rsqrt(var + EPS) * lng_ref[...] + lnb_ref[...]
    z = (z * _sigmoid(z)).astype(BF16)
    f = jnp.dot(z, wout_ref[...], preferred_element_type=F32) + bout_ref[...]
    o_ref[...] = h + _rms(f, post_ref[...])


def _conv_call(h, seq_len, pre_g, post_g, w_in, b_in, dw_w, dw_b, ln_g, ln_b,
               w_out, b_out):
    t, d = h.shape
    tm = TOKEN_TILE
    assert seq_len % tm == 0 and CONV_HALO >= CONV_WIDTH - 1
    return pl.pallas_call(
        functools.partial(_conv_kernel, seq_len // tm),
        grid=(t // tm,),
        in_specs=[
            _tile_spec(tm, d),
            _const_spec((1, d)),
            _const_spec((1, d)),
            _const_spec(w_in.shape),
            _const_spec((1, 2 * d)),
            _const_spec(dw_w.shape),
            _const_spec((1, d)),
            _const_spec((1, d)),
            _const_spec((1, d)),
            _const_spec(w_out.shape),
            _const_spec((1, d)),
        ],
        out_specs=_tile_spec(tm, d),
        out_shape=jax.ShapeDtypeStruct((t, d), F32),
        scratch_shapes=[
            pltpu.VMEM((tm + CONV_HALO, d), F32),
            pltpu.VMEM((tm, d), F32),
        ],
        compiler_params=_params(),
        name="conv_mixer",
    )(h, pre_g, post_g, w_in, b_in, dw_w, dw_b, ln_g, ln_b, w_out, b_out)


def _attn_kernel(tiles_per_seq, h_ref, pre_ref, post_ref, wqkv_ref, bqkv_ref,
                 wo_ref, bo_ref, bucket_ref, mask_ref, sink_ref, relb_ref,
                 o_ref, q_ref, kx_ref, vx_ref, ao_ref, bias_ref):
    tm, d = h_ref.shape
    nblk = tm // BLOCK
    half = V7X_LANES // 2
    i = pl.program_id(0)
    seq_start = i % tiles_per_seq == 0

    @pl.when(i == 0)
    def _():
        bucket = bucket_ref[...]

        def head_body(hh, carry):
            tile = jnp.zeros(bucket.shape, F32)
            for b in range(NUM_BUCKETS):
                tile = jnp.where(bucket == b, relb_ref[b, hh], tile)
            bias_ref[hh] = tile
            return carry

        lax.fori_loop(0, N_Q_HEADS, head_body, 0)

    @pl.when(seq_start)
    def _():
        z = jnp.zeros((N_KV_HEADS, 2, BLOCK, V7X_LANES), BF16)
        kx_ref[:, :, 0:BLOCK, :] = z
        vx_ref[:, :, 0:BLOCK, :] = z

    h = h_ref[...]
    xn = _rms(h, pre_ref[...]).astype(BF16)

    scale = HEAD_DIM ** -0.5
    qkv = jnp.dot(xn, wqkv_ref[...], preferred_element_type=F32) + bqkv_ref[...]
    for kh in range(N_KV_HEADS):
        for pr in range(2):
            c0 = (2 * kh + pr) * V7X_LANES
            qp = (qkv[:, c0:c0 + V7X_LANES] * scale).astype(BF16)
            for j in range(nblk):
                q_ref[kh, j, pr * BLOCK:(pr + 1) * BLOCK, :] = qp[j * BLOCK:(j + 1) * BLOCK, :]

    lane = lax.broadcasted_iota(jnp.int32, (tm, V7X_LANES), 1)
    low = lane < half
    for which, dst in ((0, kx_ref), (1, vx_ref)):
        base = N_Q_HEADS * HEAD_DIM + which * N_KV_HEADS * HEAD_DIM
        for col in range(N_KV_HEADS // 2):
            c0 = base + col * V7X_LANES
            x = qkv[:, c0:c0 + V7X_LANES]
            xs = pltpu.roll(x, half, axis=1)
            zero = jnp.zeros_like(x)
            kh_lo, kh_hi = 2 * col, 2 * col + 1
            dst[kh_lo, 0, BLOCK:BLOCK + tm, :] = jnp.where(low, x, zero).astype(BF16)
            dst[kh_lo, 1, BLOCK:BLOCK + tm, :] = jnp.where(low, zero, xs).astype(BF16)
            dst[kh_hi, 0, BLOCK:BLOCK + tm, :] = jnp.where(low, xs, zero).astype(BF16)
            dst[kh_hi, 1, BLOCK:BLOCK + tm, :] = jnp.where(low, zero, x).astype(BF16)

    row2 = lax.broadcasted_iota(jnp.int32, (2 * BLOCK, 1), 0)
    lane2 = lax.broadcasted_iota(jnp.int32, (2 * BLOCK, V7X_LANES), 1)
    neg_inf = jnp.float32(-jnp.inf)

    def block_body(j, carry):
        r0 = pl.multiple_of(j * BLOCK, BLOCK)
        first = jnp.logical_and(seq_start, j == 0).astype(jnp.int32)
        valid = mask_ref[first] > 0.5
        for kh in range(N_KV_HEADS):
            q2 = q_ref[kh, j]
            ls = []
            acc = None
            for eo in range(2):
                kb = kx_ref[kh, eo, pl.ds(r0, 2 * BLOCK), :]
                vb = vx_ref[kh, eo, pl.ds(r0, 2 * BLOCK), :]
                s = lax.dot_general(q2, kb, (((1,), (1,)), ((), ())),
                                    preferred_element_type=F32)
                h_top = 4 * kh + eo
                bias = jnp.concatenate([bias_ref[h_top], bias_ref[h_top + 2]], axis=0)
                s = jnp.where(valid, s + bias, neg_inf)
                sink = jnp.where(row2 < BLOCK, sink_ref[h_top], sink_ref[h_top + 2])
                m = jnp.maximum(jnp.max(s, axis=-1, keepdims=True), sink)
                p = jnp.exp(s - m)
                ls.append(jnp.sum(p, axis=-1, keepdims=True) + jnp.exp(sink - m))
                pv = jnp.dot(p.astype(BF16), vb, preferred_element_type=F32)
                acc = pv if acc is None else acc + pv
            out = acc / jnp.where(lane2 < half, ls[0], ls[1])
            out = out.astype(BF16)
            c0 = kh * 2 * V7X_LANES
            ao_ref[pl.ds(r0, BLOCK), c0:c0 + V7X_LANES] = out[:BLOCK]
            ao_ref[pl.ds(r0, BLOCK), c0 + V7X_LANES:c0 + 2 * V7X_LANES] = out[BLOCK:]
        return carry

    lax.fori_loop(0, nblk, block_body, 0)

    kx_ref[:, :, 0:BLOCK, :] = kx_ref[:, :, tm:tm + BLOCK, :]
    vx_ref[:, :, 0:BLOCK, :] = vx_ref[:, :, tm:tm + BLOCK, :]

    f = jnp.dot(ao_ref[...], wo_ref[...], preferred_element_type=F32) + bo_ref[...]
    o_ref[...] = h + _rms(f, post_ref[...])


def _t5_causal_bucket_np(dist):
    dist = np.maximum(dist, 0)
    max_exact = NUM_BUCKETS // 2
    large = max_exact + (
        np.log(np.maximum(dist, 1).astype(np.float32) / np.float32(max_exact))
        / np.float32(math.log(MAX_DISTANCE / max_exact)) * np.float32(NUM_BUCKETS - max_exact)
    ).astype(np.int32)
    large = np.minimum(large, NUM_BUCKETS - 1)
    return np.where(dist < max_exact, dist, large)


def _attn_tables():
    q_loc = np.arange(BLOCK, dtype=np.int32)[:, None] + BLOCK
    s_loc = np.arange(2 * BLOCK, dtype=np.int32)[None, :]
    dist = q_loc - s_loc
    band = (dist >= 0) & (dist < WINDOW)
    bucket = _t5_causal_bucket_np(dist).astype(np.int32)
    first = band & (s_loc >= BLOCK)
    mask = np.stack([np.tile(band, (2, 1)), np.tile(first, (2, 1))]).astype(np.float32)
    return jnp.asarray(bucket), jnp.asarray(mask)


def _attn_call(h, seq_len, pre_g, post_g, w_qkv, b_qkv, w_o, b_o, sinks, rel_bias):
    t, d = h.shape
    tm = TOKEN_TILE
    assert seq_len % tm == 0 and tm % BLOCK == 0
    nblk = tm // BLOCK
    bucket, mask = _attn_tables()
    return pl.pallas_call(
        functools.partial(_attn_kernel, seq_len // tm),
        grid=(t // tm,),
        in_specs=[
            _tile_spec(tm, d),
            _const_spec((1, d)),
            _const_spec((1, d)),
            _const_spec(w_qkv.shape),
            _const_spec((1, w_qkv.shape[1])),
            _const_spec(w_o.shape),
            _const_spec((1, d)),
            _const_spec(bucket.shape),
            _const_spec(mask.shape),
            pl.BlockSpec(memory_space=pltpu.SMEM),
            pl.BlockSpec(memory_space=pltpu.SMEM),
        ],
        out_specs=_tile_spec(tm, d),
        out_shape=jax.ShapeDtypeStruct((t, d), F32),
        scratch_shapes=[
            pltpu.VMEM((N_KV_HEADS, nblk, 2 * BLOCK, V7X_LANES), BF16),
            pltpu.VMEM((N_KV_HEADS, 2, BLOCK + tm, V7X_LANES), BF16),
            pltpu.VMEM((N_KV_HEADS, 2, BLOCK + tm, V7X_LANES), BF16),
            pltpu.VMEM((tm, d), BF16),
            pltpu.VMEM((N_Q_HEADS, BLOCK, 2 * BLOCK), F32),
        ],
        compiler_params=_params(),
        name="attn_mixer",
    )(h, pre_g, post_g, w_qkv, b_qkv, w_o, b_o, bucket, mask, sinks, rel_bias)


def kernel(x, mix_pre_g, mix_post_g, ffn_pre_g, ffn_post_g, conv_w_in, conv_b_in, conv_dw_w, conv_dw_b, conv_ln_g, conv_ln_b, conv_w_out, conv_b_out, attn_w_qkv, attn_b_qkv, attn_w_o, attn_b_o, attn_sinks, rel_bias, ffn_w_gate_up, ffn_w_down):
    b, s, d = x.shape
    depth = mix_pre_g.shape[0]
    h = x.reshape(b * s, d)
    row = lambda v: v.reshape(1, -1)
    for i in range(depth):
        j = i // 2
        if i % 2 == 0:
            h = _conv_call(
                h, s, row(mix_pre_g[i]), row(mix_post_g[i]),
                conv_w_in[j].astype(BF16), row(conv_b_in[j]), conv_dw_w[j],
                row(conv_dw_b[j]), row(conv_ln_g[j]), row(conv_ln_b[j]),
                conv_w_out[j].astype(BF16), row(conv_b_out[j]))
        else:
            h = _attn_call(
                h, s, row(mix_pre_g[i]), row(mix_post_g[i]),
                attn_w_qkv[j].astype(BF16), row(attn_b_qkv[j]),
                attn_w_o[j].astype(BF16), row(attn_b_o[j]), attn_sinks[j], rel_bias)
        h = _ffn_call(h, row(ffn_pre_g[i]), row(ffn_post_g[i]),
                      ffn_w_gate_up[i].astype(BF16), ffn_w_down[i].astype(BF16))
    return h.reshape(b, s, d)
```

```python
import functools
import math

import jax
import jax.numpy as jnp
import numpy as np
from jax import lax
from jax.experimental import pallas as pl
from jax.experimental.pallas import tpu as pltpu

D_MODEL = 1024
CONV_WIDTH = 31
HEAD_DIM = 64
N_Q_HEADS = D_MODEL // HEAD_DIM
N_KV_HEADS = 4
GROUP = N_Q_HEADS // N_KV_HEADS
WINDOW = 128
BLOCK = 128
NUM_BUCKETS = 32
MAX_DISTANCE = 128
D_FF = ((8 * D_MODEL // 3 + 255) // 256) * 256
EPS = 1e-6

V7X_LANES = 128
V7X_SUBLANES = 8
V7X_MXU_DIM = 256

TOKEN_TILE = 512
LAYER_TILE = 256
MXU_CHUNK = V7X_MXU_DIM
CONV_HALO = 32
CONV_SUBTILE = 64
VMEM_LIMIT = 56 * 1024 * 1024

F32 = jnp.float32
BF16 = jnp.bfloat16


def _rms(x, g):
    ms = jnp.mean(x * x, axis=-1, keepdims=True)
    return x * lax.rsqrt(ms + EPS) * g


def _sigmoid(x):
    return 1.0 / (1.0 + jnp.exp(-x))


def _const_spec(shape):
    zeros = (0,) * len(shape)
    return pl.BlockSpec(shape, lambda i: zeros, pipeline_mode=pl.Buffered(1))


def _tile_spec(tm, d):
    return pl.BlockSpec((tm, d), lambda i: (i, 0))


def _params(**flags):
    return pltpu.CompilerParams(
        dimension_semantics=("arbitrary",), vmem_limit_bytes=VMEM_LIMIT,
        flags=flags or None)


def _ffn_kernel(h_ref, pre_ref, post_ref, wgu_ref, wd_ref, o_ref):
    h = h_ref[...]
    xn = _rms(h, pre_ref[...]).astype(BF16)
    g = jnp.dot(xn, wgu_ref[:, :D_FF], preferred_element_type=F32)
    u = jnp.dot(xn, wgu_ref[:, D_FF:], preferred_element_type=F32)
    a = (g * _sigmoid(g) * u).astype(BF16)
    f = jnp.dot(a, wd_ref[...], preferred_element_type=F32)
    o_ref[...] = h + _rms(f, post_ref[...])


def _ffn_call(h, pre_g, post_g, w_gu, w_d):
    t, d = h.shape
    tm = TOKEN_TILE
    return pl.pallas_call(
        _ffn_kernel,
        grid=(t // tm,),
        in_specs=[
            _tile_spec(tm, d),
            _const_spec((1, d)),
            _const_spec((1, d)),
            _const_spec(w_gu.shape),
            _const_spec(w_d.shape),
        ],
        out_specs=_tile_spec(tm, d),
        out_shape=jax.ShapeDtypeStruct((t, d), F32),
        compiler_params=_params(),
        name="ffn",
    )(h, pre_g, post_g, w_gu, w_d)


def _zero_row(token):
    bits = pltpu.bitcast(token, jnp.uint32)
    bits = lax.shift_right_logical(lax.shift_right_logical(bits, jnp.uint32(16)), jnp.uint32(16))
    return pltpu.bitcast(bits, F32)[0:1, :]


class _Slots:
    def __init__(self):
        self._vec_dep = None
        self._mm_dep = None

    def run(self, vector_tasks, matmul_tasks):
        vec_tok = None
        for task in vector_tasks:
            tok = task(self._mm_dep)
            if tok is not None:
                vec_tok = tok if vec_tok is None else vec_tok + tok
        mm_tok = None
        for task in matmul_tasks:
            tok = task(self._vec_dep)
            mm_tok = tok if tok is not None else mm_tok
        self._vec_dep = None if vec_tok is None else _zero_row(vec_tok)
        self._mm_dep = None if mm_tok is None else _zero_row(mm_tok)


def _conv_layer_kernel(tiles_per_seq, xg_ref, xm_ref, pre_ref, post_ref, win_ref, bin_ref,
                       dww_ref, dwb_ref, lng_ref, lnb_ref, wout_ref, bout_ref,
                       fpre_ref, fpost_ref, wgu_ref, wd_ref,
                       o_ref, xng_ref, xnf_ref, h1_ref, a_ref, graw_ref, uraw_ref,
                       f_ref, fo_ref, z_ref, gnext_ref, ext_ref, y_ref):
    tm, d = xg_ref.shape
    s = pl.program_id(0)
    c = MXU_CHUNK
    n_glu, n_up, n_out = d // c, D_FF // c, d // c

    @pl.when(s == 0)
    def _():
        xnf_ref[...] = jnp.zeros(xnf_ref.shape, BF16)
        h1_ref[...] = jnp.zeros(h1_ref.shape, F32)
        ext_ref[...] = jnp.zeros(ext_ref.shape, F32)

    @pl.when((s - 1) % tiles_per_seq == 0)
    def _():
        ext_ref[0:CONV_HALO, :] = jnp.zeros((CONV_HALO, d), F32)

    def lhs(ref, dep):
        xn = ref[...]
        if dep is None:
            return xn
        row = dep.astype(xn.dtype)
        return xn + jnp.concatenate([row] * (xn.shape[1] // V7X_LANES), axis=1)

    def token(x):
        return x[0:V7X_SUBLANES, 0:V7X_LANES]

    def rms_g(dep):
        xng_ref[...] = _rms(xg_ref[...], pre_ref[...]).astype(BF16)

    def glu_mm(j, dep):
        xn = lhs(xng_ref, dep)
        val = jnp.dot(xn, win_ref[:, j * c:(j + 1) * c], preferred_element_type=F32)
        graw_ref[j % 4, :, 0:c] = val
        graw_ref[j % 4, :, c:2 * c] = jnp.dot(
            xn, win_ref[:, d + j * c:d + (j + 1) * c], preferred_element_type=F32)
        return token(val)

    def glu_act(j, dep):
        val = graw_ref[j % 4, :, 0:c] + bin_ref[:, j * c:(j + 1) * c]
        gate = graw_ref[j % 4, :, c:2 * c] + bin_ref[:, d + j * c:d + (j + 1) * c]
        gnext_ref[:, j * c:(j + 1) * c] = val * _sigmoid(gate)

    off0 = CONV_HALO - (CONV_WIDTH - 1)
    ts = CONV_SUBTILE
    n_rows = ts + CONV_HALO

    def conv(c0, r0, dep):
        lanes = slice(c0, c0 + V7X_LANES)
        rows = ext_ref[r0:r0 + n_rows, lanes]
        bias = dwb_ref[:, lanes] if dep is None else dwb_ref[:, lanes] + dep
        acc = jnp.broadcast_to(bias, (ts, V7X_LANES))
        for r in range(V7X_SUBLANES):
            offs = [o for o in range(off0, off0 + CONV_WIDTH) if o % V7X_SUBLANES == r]
            xr = rows if r == 0 else pltpu.roll(rows, n_rows - r, axis=0)
            for o in offs:
                a = o - r
                acc = acc + xr[a:a + ts, :] * dww_ref[o - off0:o - off0 + 1, lanes]
        y_ref[r0:r0 + ts, lanes] = acc
        return sum(acc[i:i + V7X_SUBLANES] for i in range(0, ts, V7X_SUBLANES))

    convs = [functools.partial(conv, c0, r0)
             for c0 in range(0, d, V7X_LANES) for r0 in range(0, tm, ts)]

    def layer_norm_act(dep):
        y = y_ref[...]
        mu = jnp.mean(y, axis=-1, keepdims=True)
        yc = y - mu
        var = jnp.mean(yc * yc, axis=-1, keepdims=True)
        z = yc * lax.rsqrt(var + EPS) * lng_ref[...] + lnb_ref[...]
        z = z * _sigmoid(z)
        z_ref[...] = z.astype(BF16)
        return token(z)

    def out_mm(j, dep):
        w = 2 * c
        fo = jnp.dot(lhs(z_ref, dep), wout_ref[:, j * w:(j + 1) * w], preferred_element_type=F32)
        fo_ref[:, j * w:(j + 1) * w] = fo
        return token(fo)

    def refill_ext(dep):
        ext_ref[0:CONV_HALO, :] = ext_ref[tm:tm + CONV_HALO, :]
        ext_ref[CONV_HALO:CONV_HALO + tm, :] = gnext_ref[...]

    def mixer_finish(dep):
        h1 = xm_ref[...] + _rms(fo_ref[...] + bout_ref[...], post_ref[...])
        h1_ref[...] = h1
        xnf_ref[...] = _rms(h1, fpre_ref[...]).astype(BF16)

    def up_mm(j, dep):
        xn = lhs(xnf_ref, dep)
        g = jnp.dot(xn, wgu_ref[:, j * c:(j + 1) * c], preferred_element_type=F32)
        uraw_ref[j % 4, :, 0:c] = g
        uraw_ref[j % 4, :, c:2 * c] = jnp.dot(
            xn, wgu_ref[:, D_FF + j * c:D_FF + (j + 1) * c], preferred_element_type=F32)
        return token(g)

    def up_act(j, dep):
        g = uraw_ref[j % 4, :, 0:c]
        u = uraw_ref[j % 4, :, c:2 * c]
        a_ref[:, j * c:(j + 1) * c] = (g * _sigmoid(g) * u).astype(BF16)

    def down_mm(j, dep):
        w = 2 * c
        f = jnp.dot(lhs(a_ref, dep), wd_ref[:, j * w:(j + 1) * w], preferred_element_type=F32)
        f_ref[:, j * w:(j + 1) * w] = f
        return token(f)

    def ffn_finish(dep):
        o_ref[...] = h1_ref[...] + _rms(f_ref[...], fpost_ref[...])

    P = functools.partial
    slots = _Slots()
    per = -(-len(convs) // 8)
    cv = [convs[i * per:(i + 1) * per] for i in range(8)]
    assert n_up == 11 and n_glu == 4 and n_out == 4
    slots.run([rms_g] + cv[0], [P(up_mm, 0), P(up_mm, 1)])
    slots.run([P(up_act, 0), P(up_act, 1)] + cv[1], [P(up_mm, 2), P(up_mm, 3)])
    slots.run([P(up_act, 2), P(up_act, 3)] + cv[2], [P(up_mm, 4), P(up_mm, 5)])
    slots.run([P(up_act, 4), P(up_act, 5)] + cv[3], [P(up_mm, 6), P(up_mm, 7)])
    slots.run([P(up_act, 6), P(up_act, 7)] + cv[4], [P(up_mm, 8), P(up_mm, 9)])
    slots.run([P(up_act, 8), P(up_act, 9)] + cv[5], [P(up_mm, 10), P(glu_mm, 0)])
    slots.run([P(up_act, 10), P(glu_act, 0)] + cv[6], [P(glu_mm, 1), P(glu_mm, 2)])
    slots.run([P(glu_act, 1), P(glu_act, 2)] + cv[7], [P(glu_mm, 3), P(down_mm, 0)])
    slots.run([P(glu_act, 3), layer_norm_act], [P(down_mm, 1)])
    slots.run([refill_ext], [P(out_mm, 0), P(out_mm, 1)])
    slots.run([ffn_finish, mixer_finish], [])


def _conv_layer_call(h, seq_len, mixer_ops, ffn_ops):
    t, d = h.shape
    tm = LAYER_TILE
    assert seq_len % tm == 0 and t % seq_len == 0 and CONV_HALO >= CONV_WIDTH - 1
    n = t // tm
    ops = list(mixer_ops) + list(ffn_ops)
    return pl.pallas_call(
        functools.partial(_conv_layer_kernel, seq_len // tm),
        grid=(n + 2,),
        in_specs=[pl.BlockSpec((tm, d), lambda s: (jnp.minimum(s, n - 1), 0)),
                  pl.BlockSpec((tm, d), lambda s: (jnp.clip(s - 1, 0, n - 1), 0))]
        + [_const_spec(o.shape) for o in ops],
        out_specs=pl.BlockSpec((tm, d), lambda s: (jnp.clip(s - 2, 0, n - 1), 0)),
        out_shape=jax.ShapeDtypeStruct((t, d), F32),
        scratch_shapes=[
            pltpu.VMEM((tm, d), BF16),
            pltpu.VMEM((tm, d), BF16),
            pltpu.VMEM((tm, d), F32),
            pltpu.VMEM((tm, D_FF), BF16),
            pltpu.VMEM((4, tm, 2 * MXU_CHUNK), F32),
            pltpu.VMEM((4, tm, 2 * MXU_CHUNK), F32),
            pltpu.VMEM((tm, d), F32),
            pltpu.VMEM((tm, d), F32),
            pltpu.VMEM((tm, d), BF16),
            pltpu.VMEM((tm, d), F32),
            pltpu.VMEM((tm + CONV_HALO, d), F32),
            pltpu.VMEM((tm, d), F32),
        ],
        compiler_params=_params(),
        name="conv_layer",
    )(h, h, *ops)


def _attn_kernel(tiles_per_seq, h_ref, pre_ref, post_ref, wqkv_ref, bqkv_ref,
                 wo_ref, bo_ref, bucket_ref, mask_ref, sink_ref, relb_ref,
                 o_ref, q_ref, kx_ref, vx_ref, ao_ref, bias_ref):
    tm, d = h_ref.shape
    nblk = tm // BLOCK
    half = V7X_LANES // 2
    i = pl.program_id(0)
    seq_start = i % tiles_per_seq == 0

    @pl.when(i == 0)
    def _():
        bucket = bucket_ref[...]

        def head_body(hh, carry):
            tile = jnp.zeros(bucket.shape, F32)
            for b in range(NUM_BUCKETS):
                tile = jnp.where(bucket == b, relb_ref[b, hh], tile)
            bias_ref[hh] = tile
            return carry

        lax.fori_loop(0, N_Q_HEADS, head_body, 0)

    @pl.when(seq_start)
    def _():
        z = jnp.zeros((N_KV_HEADS, 2, BLOCK, V7X_LANES), BF16)
        kx_ref[:, :, 0:BLOCK, :] = z
        vx_ref[:, :, 0:BLOCK, :] = z

    h = h_ref[...]
    xn = _rms(h, pre_ref[...]).astype(BF16)

    scale = HEAD_DIM ** -0.5
    qkv = jnp.dot(xn, wqkv_ref[...], preferred_element_type=F32) + bqkv_ref[...]
    for kh in range(N_KV_HEADS):
        for pr in range(2):
            c0 = (2 * kh + pr) * V7X_LANES
            qp = (qkv[:, c0:c0 + V7X_LANES] * scale).astype(BF16)
            for j in range(nblk):
                q_ref[kh, j, pr * BLOCK:(pr + 1) * BLOCK, :] = qp[j * BLOCK:(j + 1) * BLOCK, :]

    lane = lax.broadcasted_iota(jnp.int32, (tm, V7X_LANES), 1)
    low = lane < half
    for which, dst in ((0, kx_ref), (1, vx_ref)):
        base = N_Q_HEADS * HEAD_DIM + which * N_KV_HEADS * HEAD_DIM
        for col in range(N_KV_HEADS // 2):
            c0 = base + col * V7X_LANES
            x = qkv[:, c0:c0 + V7X_LANES]
            xs = pltpu.roll(x, half, axis=1)
            zero = jnp.zeros_like(x)
            kh_lo, kh_hi = 2 * col, 2 * col + 1
            dst[kh_lo, 0, BLOCK:BLOCK + tm, :] = jnp.where(low, x, zero).astype(BF16)
            dst[kh_lo, 1, BLOCK:BLOCK + tm, :] = jnp.where(low, zero, xs).astype(BF16)
            dst[kh_hi, 0, BLOCK:BLOCK + tm, :] = jnp.where(low, xs, zero).astype(BF16)
            dst[kh_hi, 1, BLOCK:BLOCK + tm, :] = jnp.where(low, zero, x).astype(BF16)

    row2 = lax.broadcasted_iota(jnp.int32, (2 * BLOCK, 1), 0)
    lane2 = lax.broadcasted_iota(jnp.int32, (2 * BLOCK, V7X_LANES), 1)
    neg_inf = jnp.float32(-jnp.inf)

    def block_body(j, carry):
        r0 = pl.multiple_of(j * BLOCK, BLOCK)
        first = jnp.logical_and(seq_start, j == 0).astype(jnp.int32)
        valid = mask_ref[first] > 0.5
        for kh in range(N_KV_HEADS):
            q2 = q_ref[kh, j]
            ls = []
            acc = None
            for eo in range(2):
                kb = kx_ref[kh, eo, pl.ds(r0, 2 * BLOCK), :]
                vb = vx_ref[kh, eo, pl.ds(r0, 2 * BLOCK), :]
                s = lax.dot_general(q2, kb, (((1,), (1,)), ((), ())),
                                    preferred_element_type=F32)
                h_top = 4 * kh + eo
                bias = jnp.concatenate([bias_ref[h_top], bias_ref[h_top + 2]], axis=0)
                s = jnp.where(valid, s + bias, neg_inf)
                sink = jnp.where(row2 < BLOCK, sink_ref[h_top], sink_ref[h_top + 2])
                m = jnp.maximum(jnp.max(s, axis=-1, keepdims=True), sink)
                p = jnp.exp(s - m)
                ls.append(jnp.sum(p, axis=-1, keepdims=True) + jnp.exp(sink - m))
                pv = jnp.dot(p.astype(BF16), vb, preferred_element_type=F32)
                acc = pv if acc is None else acc + pv
            out = acc / jnp.where(lane2 < half, ls[0], ls[1])
            out = out.astype(BF16)
            c0 = kh * 2 * V7X_LANES
            ao_ref[pl.ds(r0, BLOCK), c0:c0 + V7X_LANES] = out[:BLOCK]
            ao_ref[pl.ds(r0, BLOCK), c0 + V7X_LANES:c0 + 2 * V7X_LANES] = out[BLOCK:]
        return carry

    lax.fori_loop(0, nblk, block_body, 0)

    kx_ref[:, :, 0:BLOCK, :] = kx_ref[:, :, tm:tm + BLOCK, :]
    vx_ref[:, :, 0:BLOCK, :] = vx_ref[:, :, tm:tm + BLOCK, :]

    f = jnp.dot(ao_ref[...], wo_ref[...], preferred_element_type=F32) + bo_ref[...]
    o_ref[...] = h + _rms(f, post_ref[...])


def _t5_causal_bucket_np(dist):
    dist = np.maximum(dist, 0)
    max_exact = NUM_BUCKETS // 2
    large = max_exact + (
        np.log(np.maximum(dist, 1).astype(np.float32) / np.float32(max_exact))
        / np.float32(math.log(MAX_DISTANCE / max_exact)) * np.float32(NUM_BUCKETS - max_exact)
    ).astype(np.int32)
    large = np.minimum(large, NUM_BUCKETS - 1)
    return np.where(dist < max_exact, dist, large)


def _attn_tables():
    q_loc = np.arange(BLOCK, dtype=np.int32)[:, None] + BLOCK
    s_loc = np.arange(2 * BLOCK, dtype=np.int32)[None, :]
    dist = q_loc - s_loc
    band = (dist >= 0) & (dist < WINDOW)
    bucket = _t5_causal_bucket_np(dist).astype(np.int32)
    first = band & (s_loc >= BLOCK)
    mask = np.stack([np.tile(band, (2, 1)), np.tile(first, (2, 1))]).astype(np.float32)
    return jnp.asarray(bucket), jnp.asarray(mask)


def _attn_call(h, seq_len, pre_g, post_g, w_qkv, b_qkv, w_o, b_o, sinks, rel_bias):
    t, d = h.shape
    tm = TOKEN_TILE
    assert seq_len % tm == 0 and tm % BLOCK == 0
    nblk = tm // BLOCK
    bucket, mask = _attn_tables()
    return pl.pallas_call(
        functools.partial(_attn_kernel, seq_len // tm),
        grid=(t // tm,),
        in_specs=[
            _tile_spec(tm, d),
            _const_spec((1, d)),
            _const_spec((1, d)),
            _const_spec(w_qkv.shape),
            _const_spec((1, w_qkv.shape[1])),
            _const_spec(w_o.shape),
            _const_spec((1, d)),
            _const_spec(bucket.shape),
            _const_spec(mask.shape),
            pl.BlockSpec(memory_space=pltpu.SMEM),
            pl.BlockSpec(memory_space=pltpu.SMEM),
        ],
        out_specs=_tile_spec(tm, d),
        out_shape=jax.ShapeDtypeStruct((t, d), F32),
        scratch_shapes=[
            pltpu.VMEM((N_KV_HEADS, nblk, 2 * BLOCK, V7X_LANES), BF16),
            pltpu.VMEM((N_KV_HEADS, 2, BLOCK + tm, V7X_LANES), BF16),
            pltpu.VMEM((N_KV_HEADS, 2, BLOCK + tm, V7X_LANES), BF16),
            pltpu.VMEM((tm, d), BF16),
            pltpu.VMEM((N_Q_HEADS, BLOCK, 2 * BLOCK), F32),
        ],
        compiler_params=_params(),
        name="attn_mixer",
    )(h, pre_g, post_g, w_qkv, b_qkv, w_o, b_o, bucket, mask, sinks, rel_bias)


def kernel(x, mix_pre_g, mix_post_g, ffn_pre_g, ffn_post_g, conv_w_in, conv_b_in, conv_dw_w, conv_dw_b, conv_ln_g, conv_ln_b, conv_w_out, conv_b_out, attn_w_qkv, attn_b_qkv, attn_w_o, attn_b_o, attn_sinks, rel_bias, ffn_w_gate_up, ffn_w_down):
    b, s, d = x.shape
    depth = mix_pre_g.shape[0]
    h = x.reshape(b * s, d)
    row = lambda v: v.reshape(1, -1)
    for i in range(depth):
        j = i // 2
        ffn_ops = [row(ffn_pre_g[i]), row(ffn_post_g[i]),
                   ffn_w_gate_up[i].astype(BF16), ffn_w_down[i].astype(BF16)]
        if i % 2 == 0:
            mixer_ops = [row(mix_pre_g[i]), row(mix_post_g[i]),
                         conv_w_in[j].astype(BF16), row(conv_b_in[j]), conv_dw_w[j],
                         row(conv_dw_b[j]), row(conv_ln_g[j]), row(conv_ln_b[j]),
                         conv_w_out[j].astype(BF16), row(conv_b_out[j])]
            h = _conv_layer_call(h, s, mixer_ops, ffn_ops)
        else:
            h = _attn_call(
                h, s, row(mix_pre_g[i]), row(mix_post_g[i]),
                attn_w_qkv[j].astype(BF16), row(attn_b_qkv[j]),
                attn_w_o[j].astype(BF16), row(attn_b_o[j]), attn_sinks[j], rel_bias)
            h = _ffn_call(h, *ffn_ops)
    return h.reshape(b, s, d)
```

```python
import functools
import math

import jax
import jax.numpy as jnp
import numpy as np
from jax import lax
from jax.experimental import pallas as pl
from jax.experimental.pallas import tpu as pltpu

D_MODEL = 1024
CONV_WIDTH = 31
HEAD_DIM = 64
N_Q_HEADS = D_MODEL // HEAD_DIM
N_KV_HEADS = 4
GROUP = N_Q_HEADS // N_KV_HEADS
WINDOW = 128
BLOCK = 128
NUM_BUCKETS = 32
MAX_DISTANCE = 128
D_FF = ((8 * D_MODEL // 3 + 255) // 256) * 256
EPS = 1e-6

V7X_LANES = 128
V7X_SUBLANES = 8
V7X_MXU_DIM = 256

TOKEN_TILE = 512
LAYER_TILE = 256
MXU_CHUNK = V7X_MXU_DIM
CONV_HALO = 32
CONV_SUBTILE = 64
VMEM_LIMIT = 56 * 1024 * 1024

F32 = jnp.float32
BF16 = jnp.bfloat16


def _rms(x, g):
    ms = jnp.mean(x * x, axis=-1, keepdims=True)
    return x * lax.rsqrt(ms + EPS) * g


def _sigmoid(x):
    return 1.0 / (1.0 + jnp.exp(-x))


def _const_spec(shape):
    zeros = (0,) * len(shape)
    return pl.BlockSpec(shape, lambda i: zeros, pipeline_mode=pl.Buffered(1))


def _tile_spec(tm, d):
    return pl.BlockSpec((tm, d), lambda i: (i, 0))


def _params(**flags):
    return pltpu.CompilerParams(
        dimension_semantics=("arbitrary",), vmem_limit_bytes=VMEM_LIMIT,
        flags=flags or None)


def _ffn_kernel(h_ref, pre_ref, post_ref, wgu_ref, wd_ref, o_ref):
    h = h_ref[...]
    xn = _rms(h, pre_ref[...]).astype(BF16)
    g = jnp.dot(xn, wgu_ref[:, :D_FF], preferred_element_type=F32)
    u = jnp.dot(xn, wgu_ref[:, D_FF:], preferred_element_type=F32)
    a = (g * _sigmoid(g) * u).astype(BF16)
    f = jnp.dot(a, wd_ref[...], preferred_element_type=F32)
    o_ref[...] = h + _rms(f, post_ref[...])


def _ffn_call(h, pre_g, post_g, w_gu, w_d):
    t, d = h.shape
    tm = TOKEN_TILE
    return pl.pallas_call(
        _ffn_kernel,
        grid=(t // tm,),
        in_specs=[
            _tile_spec(tm, d),
            _const_spec((1, d)),
            _const_spec((1, d)),
            _const_spec(w_gu.shape),
            _const_spec(w_d.shape),
        ],
        out_specs=_tile_spec(tm, d),
        out_shape=jax.ShapeDtypeStruct((t, d), F32),
        compiler_params=_params(),
        name="ffn",
    )(h, pre_g, post_g, w_gu, w_d)


def _zero_row(token):
    bits = pltpu.bitcast(token, jnp.uint32)
    bits = lax.shift_right_logical(lax.shift_right_logical(bits, jnp.uint32(16)), jnp.uint32(16))
    return pltpu.bitcast(bits, F32)[0:1, :]


class _Slots:
    def __init__(self):
        self._vec_dep = None
        self._mm_dep = None

    def run(self, vector_tasks, matmul_tasks):
        vec_tok = None
        for task in vector_tasks:
            tok = task(self._mm_dep)
            if tok is not None:
                vec_tok = tok if vec_tok is None else vec_tok + tok
        mm_tok = None
        for task in matmul_tasks:
            tok = task(self._vec_dep)
            mm_tok = tok if tok is not None else mm_tok
        self._vec_dep = None if vec_tok is None else _zero_row(vec_tok)
        self._mm_dep = None if mm_tok is None else _zero_row(mm_tok)


def _conv_layer_kernel(tiles_per_seq, xg_ref, xm_ref, pre_ref, post_ref, win_ref, bin_ref,
                       dww_ref, dwb_ref, lng_ref, lnb_ref, wout_ref, bout_ref,
                       fpre_ref, fpost_ref, wgu_ref, wd_ref,
                       o_ref, xng_ref, xnf_ref, h1_ref, a_ref, graw_ref, uraw_ref,
                       f_ref, fo_ref, z_ref, gnext_ref, ext_ref, y_ref):
    tm, d = xg_ref.shape
    s = pl.program_id(0)
    c = MXU_CHUNK
    n_glu, n_up, n_out = d // c, D_FF // c, d // c

    @pl.when(s == 0)
    def _():
        xnf_ref[...] = jnp.zeros(xnf_ref.shape, BF16)
        h1_ref[...] = jnp.zeros(h1_ref.shape, F32)
        ext_ref[...] = jnp.zeros(ext_ref.shape, F32)

    @pl.when((s - 1) % tiles_per_seq == 0)
    def _():
        ext_ref[0:CONV_HALO, :] = jnp.zeros((CONV_HALO, d), F32)

    def lhs(ref, dep):
        xn = ref[...]
        if dep is None:
            return xn
        row = dep.astype(xn.dtype)
        return xn + jnp.concatenate([row] * (xn.shape[1] // V7X_LANES), axis=1)

    def token(x):
        return x[0:V7X_SUBLANES, 0:V7X_LANES]

    def rms_g(dep):
        xng_ref[...] = _rms(xg_ref[...], pre_ref[...]).astype(BF16)

    def glu_mm(j, dep):
        xn = lhs(xng_ref, dep)
        val = jnp.dot(xn, win_ref[:, j * c:(j + 1) * c], preferred_element_type=F32)
        graw_ref[j % 4, :, 0:c] = val
        graw_ref[j % 4, :, c:2 * c] = jnp.dot(
            xn, win_ref[:, d + j * c:d + (j + 1) * c], preferred_element_type=F32)
        return token(val)

    def glu_act(j, dep):
        val = graw_ref[j % 4, :, 0:c] + bin_ref[:, j * c:(j + 1) * c]
        gate = graw_ref[j % 4, :, c:2 * c] + bin_ref[:, d + j * c:d + (j + 1) * c]
        gnext_ref[:, j * c:(j + 1) * c] = val * _sigmoid(gate)

    off0 = CONV_HALO - (CONV_WIDTH - 1)
    ts = CONV_SUBTILE
    n_rows = ts + CONV_HALO

    def conv(c0, r0, dep):
        lanes = slice(c0, c0 + V7X_LANES)
        rows = ext_ref[r0:r0 + n_rows, lanes]
        bias = dwb_ref[:, lanes] if dep is None else dwb_ref[:, lanes] + dep
        acc = jnp.broadcast_to(bias, (ts, V7X_LANES))
        for r in range(V7X_SUBLANES):
            offs = [o for o in range(off0, off0 + CONV_WIDTH) if o % V7X_SUBLANES == r]
            xr = rows if r == 0 else pltpu.roll(rows, n_rows - r, axis=0)
            for o in offs:
                a = o - r
                acc = acc + xr[a:a + ts, :] * dww_ref[o - off0:o - off0 + 1, lanes]
        y_ref[r0:r0 + ts, lanes] = acc
        return sum(acc[i:i + V7X_SUBLANES] for i in range(0, ts, V7X_SUBLANES))

    convs = [functools.partial(conv, c0, r0)
             for c0 in range(0, d, V7X_LANES) for r0 in range(0, tm, ts)]

    def layer_norm_act(dep):
        y = y_ref[...]
        mu = jnp.mean(y, axis=-1, keepdims=True)
        yc = y - mu
        var = jnp.mean(yc * yc, axis=-1, keepdims=True)
        z = yc * lax.rsqrt(var + EPS) * lng_ref[...] + lnb_ref[...]
        z = z * _sigmoid(z)
        z_ref[...] = z.astype(BF16)
        return token(z)

    def out_mm(j, dep):
        w = 2 * c
        fo = jnp.dot(lhs(z_ref, dep), wout_ref[:, j * w:(j + 1) * w], preferred_element_type=F32)
        fo_ref[:, j * w:(j + 1) * w] = fo
        return token(fo)

    def refill_ext(dep):
        ext_ref[0:CONV_HALO, :] = ext_ref[tm:tm + CONV_HALO, :]
        ext_ref[CONV_HALO:CONV_HALO + tm, :] = gnext_ref[...]

    def mixer_finish(dep):
        h1 = xm_ref[...] + _rms(fo_ref[...] + bout_ref[...], post_ref[...])
        h1_ref[...] = h1
        xnf_ref[...] = _rms(h1, fpre_ref[...]).astype(BF16)

    def up_mm(j, dep):
        xn = lhs(xnf_ref, dep)
        g = jnp.dot(xn, wgu_ref[:, j * c:(j + 1) * c], preferred_element_type=F32)
        uraw_ref[j % 4, :, 0:c] = g
        uraw_ref[j % 4, :, c:2 * c] = jnp.dot(
            xn, wgu_ref[:, D_FF + j * c:D_FF + (j + 1) * c], preferred_element_type=F32)
        return token(g)

    def up_act(j, dep):
        g = uraw_ref[j % 4, :, 0:c]
        u = uraw_ref[j % 4, :, c:2 * c]
        a_ref[:, j * c:(j + 1) * c] = (g * _sigmoid(g) * u).astype(BF16)

    def down_mm(j, dep):
        w = 2 * c
        f = jnp.dot(lhs(a_ref, dep), wd_ref[:, j * w:(j + 1) * w], preferred_element_type=F32)
        f_ref[:, j * w:(j + 1) * w] = f
        return token(f)

    def ffn_finish(dep):
        o_ref[...] = h1_ref[...] + _rms(f_ref[...], fpost_ref[...])

    P = functools.partial
    slots = _Slots()
    per = -(-len(convs) // 8)
    cv = [convs[i * per:(i + 1) * per] for i in range(8)]
    assert n_up == 11 and n_glu == 4 and n_out == 4
    slots.run([rms_g] + cv[0], [P(up_mm, 0), P(up_mm, 1)])
    slots.run([P(up_act, 0), P(up_act, 1)] + cv[1], [P(up_mm, 2), P(up_mm, 3)])
    slots.run([P(up_act, 2), P(up_act, 3)] + cv[2], [P(up_mm, 4), P(up_mm, 5)])
    slots.run([P(up_act, 4), P(up_act, 5)] + cv[3], [P(up_mm, 6), P(up_mm, 7)])
    slots.run([P(up_act, 6), P(up_act, 7)] + cv[4], [P(up_mm, 8), P(up_mm, 9)])
    slots.run([P(up_act, 8), P(up_act, 9)] + cv[5], [P(up_mm, 10), P(glu_mm, 0)])
    slots.run([P(up_act, 10), P(glu_act, 0)] + cv[6], [P(glu_mm, 1), P(glu_mm, 2)])
    slots.run([P(glu_act, 1), P(glu_act, 2)] + cv[7], [P(glu_mm, 3), P(down_mm, 0)])
    slots.run([P(glu_act, 3), layer_norm_act], [P(down_mm, 1)])
    slots.run([refill_ext], [P(out_mm, 0), P(out_mm, 1)])
    slots.run([ffn_finish, mixer_finish], [])


def _conv_layer_call(h, seq_len, mixer_ops, ffn_ops):
    t, d = h.shape
    tm = LAYER_TILE
    assert seq_len % tm == 0 and t % seq_len == 0 and CONV_HALO >= CONV_WIDTH - 1
    n = t // tm
    ops = list(mixer_ops) + list(ffn_ops)
    return pl.pallas_call(
        functools.partial(_conv_layer_kernel, seq_len // tm),
        grid=(n + 2,),
        in_specs=[pl.BlockSpec((tm, d), lambda s: (jnp.minimum(s, n - 1), 0)),
                  pl.BlockSpec((tm, d), lambda s: (jnp.clip(s - 1, 0, n - 1), 0))]
        + [_const_spec(o.shape) for o in ops],
        out_specs=pl.BlockSpec((tm, d), lambda s: (jnp.clip(s - 2, 0, n - 1), 0)),
        out_shape=jax.ShapeDtypeStruct((t, d), F32),
        scratch_shapes=[
            pltpu.VMEM((tm, d), BF16),
            pltpu.VMEM((tm, d), BF16),
            pltpu.VMEM((tm, d), F32),
            pltpu.VMEM((tm, D_FF), BF16),
            pltpu.VMEM((4, tm, 2 * MXU_CHUNK), F32),
            pltpu.VMEM((4, tm, 2 * MXU_CHUNK), F32),
            pltpu.VMEM((tm, d), F32),
            pltpu.VMEM((tm, d), F32),
            pltpu.VMEM((tm, d), BF16),
            pltpu.VMEM((tm, d), F32),
            pltpu.VMEM((tm + CONV_HALO, d), F32),
            pltpu.VMEM((tm, d), F32),
        ],
        compiler_params=_params(),
        name="conv_layer",
    )(h, h, *ops)


def _attn_kernel(tiles_per_seq, h_ref, pre_ref, post_ref, wqkv_ref, bqkv_ref,
                 wo_ref, bo_ref, bucket_ref, cap_ref, sink_ref, relb_ref,
                 o_ref, q_ref, kw_ref, vw_ref, kc_ref, vc_ref, ao_ref, bias_ref):
    tm, d = h_ref.shape
    nblk = tm // BLOCK
    half = V7X_LANES // 2
    i = pl.program_id(0)
    seq_start = i % tiles_per_seq == 0

    @pl.when(i == 0)
    def _():
        vw_ref[:, :, :, :, V7X_LANES:] = jnp.ones(
            (N_KV_HEADS, 2, nblk, 2 * BLOCK, V7X_LANES), BF16)
        bucket = bucket_ref[...]
        key_col = lax.broadcasted_iota(jnp.int32, bucket.shape, 1)

        def head_body(hh, carry):
            tile = jnp.zeros(bucket.shape, F32)
            for b in range(NUM_BUCKETS):
                tile = jnp.where(bucket == b, relb_ref[b, hh], tile)
            bias_ref[hh] = jnp.where(key_col == 0, sink_ref[hh], tile)
            return carry

        lax.fori_loop(0, N_Q_HEADS, head_body, 0)

    @pl.when(seq_start)
    def _():
        z = jnp.zeros((N_KV_HEADS, 2, BLOCK, V7X_LANES), BF16)
        kc_ref[...] = z
        vc_ref[...] = z

    h = h_ref[...]
    xn = _rms(h, pre_ref[...]).astype(BF16)

    scale = HEAD_DIM ** -0.5
    qkv = jnp.dot(xn, wqkv_ref[...], preferred_element_type=F32) + bqkv_ref[...]
    for kh in range(N_KV_HEADS):
        for pr in range(2):
            c0 = (2 * kh + pr) * V7X_LANES
            qp = (qkv[:, c0:c0 + V7X_LANES] * scale).astype(BF16)
            for j in range(nblk):
                q_ref[kh, j, pr * BLOCK:(pr + 1) * BLOCK, :] = qp[j * BLOCK:(j + 1) * BLOCK, :]

    lane = lax.broadcasted_iota(jnp.int32, (tm, V7X_LANES), 1)
    low = lane < half
    row = lax.broadcasted_iota(jnp.int32, (tm, V7X_LANES), 0)
    block_start = row % BLOCK == 0
    for which, win_ref, carry_ref in ((0, kw_ref, kc_ref), (1, vw_ref, vc_ref)):
        base = N_Q_HEADS * HEAD_DIM + which * N_KV_HEADS * HEAD_DIM
        for col in range(N_KV_HEADS // 2):
            c0 = base + col * V7X_LANES
            x = qkv[:, c0:c0 + V7X_LANES]
            xs = pltpu.roll(x, half, axis=1)
            zero = jnp.zeros_like(x)
            variants = {(2 * col, 0): jnp.where(low, x, zero),
                        (2 * col, 1): jnp.where(low, zero, xs),
                        (2 * col + 1, 0): jnp.where(low, xs, zero),
                        (2 * col + 1, 1): jnp.where(low, zero, x)}
            for (kh, var), val in variants.items():
                cur = val.astype(BF16)
                prv = jnp.where(block_start, zero, val).astype(BF16)
                win_ref[kh, var, 0, 0:BLOCK, 0:V7X_LANES] = carry_ref[kh, var]
                for j in range(nblk):
                    blk = slice(j * BLOCK, (j + 1) * BLOCK)
                    win_ref[kh, var, j, BLOCK:2 * BLOCK, 0:V7X_LANES] = cur[blk]
                    if j + 1 < nblk:
                        win_ref[kh, var, j + 1, 0:BLOCK, 0:V7X_LANES] = prv[blk]
                carry_ref[kh, var] = prv[(nblk - 1) * BLOCK:]

    lane2 = lax.broadcasted_iota(jnp.int32, (2 * BLOCK, V7X_LANES), 1)

    def block_body(j, carry):
        r0 = pl.multiple_of(j * BLOCK, BLOCK)
        first = jnp.logical_and(seq_start, j == 0).astype(jnp.int32)
        for kh in range(N_KV_HEADS):
            q2 = q_ref[kh, j]
            ls = []
            acc = None
            for eo in range(2):
                kb = kw_ref[kh, eo, j]
                vb = vw_ref[kh, eo, j]
                s = lax.dot_general(q2, kb, (((1,), (1,)), ((), ())),
                                    preferred_element_type=F32)
                h_top = 4 * kh + eo
                bias = jnp.concatenate([bias_ref[h_top], bias_ref[h_top + 2]], axis=0)
                s = jnp.minimum(s + bias, cap_ref[first])
                m = jnp.max(s, axis=-1, keepdims=True)
                p = jnp.exp(s - m).astype(BF16)
                pv = jnp.dot(p, vb, preferred_element_type=F32)
                ls.append(pv[:, V7X_LANES:])
                acc = pv[:, :V7X_LANES] if acc is None else acc + pv[:, :V7X_LANES]
            out = acc / jnp.where(lane2 < half, ls[0], ls[1])
            out = out.astype(BF16)
            c0 = kh * 2 * V7X_LANES
            ao_ref[pl.ds(r0, BLOCK), c0:c0 + V7X_LANES] = out[:BLOCK]
            ao_ref[pl.ds(r0, BLOCK), c0 + V7X_LANES:c0 + 2 * V7X_LANES] = out[BLOCK:]
        return carry

    lax.fori_loop(0, nblk, block_body, 0, unroll=True)

    f = jnp.dot(ao_ref[...], wo_ref[...], preferred_element_type=F32) + bo_ref[...]
    o_ref[...] = h + _rms(f, post_ref[...])


def _t5_causal_bucket_np(dist):
    dist = np.maximum(dist, 0)
    max_exact = NUM_BUCKETS // 2
    large = max_exact + (
        np.log(np.maximum(dist, 1).astype(np.float32) / np.float32(max_exact))
        / np.float32(math.log(MAX_DISTANCE / max_exact)) * np.float32(NUM_BUCKETS - max_exact)
    ).astype(np.int32)
    large = np.minimum(large, NUM_BUCKETS - 1)
    return np.where(dist < max_exact, dist, large)


def _attn_tables():
    q_loc = np.arange(BLOCK, dtype=np.int32)[:, None] + BLOCK
    s_loc = np.arange(2 * BLOCK, dtype=np.int32)[None, :]
    dist = q_loc - s_loc
    band = (dist >= 0) & (dist < WINDOW)
    bucket = _t5_causal_bucket_np(dist).astype(np.int32)
    first = band & (s_loc >= BLOCK)
    valid = np.stack([np.tile(band, (2, 1)), np.tile(first, (2, 1))])
    assert not valid[:, :, 0].any()
    valid[:, :, 0] = True
    cap = np.where(valid, np.inf, -np.inf).astype(np.float32)
    return jnp.asarray(bucket), jnp.asarray(cap)


def _attn_call(h, seq_len, pre_g, post_g, w_qkv, b_qkv, w_o, b_o, sinks, rel_bias):
    t, d = h.shape
    tm = TOKEN_TILE
    assert seq_len % tm == 0 and tm % BLOCK == 0
    nblk = tm // BLOCK
    bucket, cap = _attn_tables()
    return pl.pallas_call(
        functools.partial(_attn_kernel, seq_len // tm),
        grid=(t // tm,),
        in_specs=[
            _tile_spec(tm, d),
            _const_spec((1, d)),
            _const_spec((1, d)),
            _const_spec(w_qkv.shape),
            _const_spec((1, w_qkv.shape[1])),
            _const_spec(w_o.shape),
            _const_spec((1, d)),
            _const_spec(bucket.shape),
            _const_spec(cap.shape),
            pl.BlockSpec(memory_space=pltpu.SMEM),
            pl.BlockSpec(memory_space=pltpu.SMEM),
        ],
        out_specs=_tile_spec(tm, d),
        out_shape=jax.ShapeDtypeStruct((t, d), F32),
        scratch_shapes=[
            pltpu.VMEM((N_KV_HEADS, nblk, 2 * BLOCK, V7X_LANES), BF16),
            pltpu.VMEM((N_KV_HEADS, 2, nblk, 2 * BLOCK, V7X_LANES), BF16),
            pltpu.VMEM((N_KV_HEADS, 2, nblk, 2 * BLOCK, 2 * V7X_LANES), BF16),
            pltpu.VMEM((N_KV_HEADS, 2, BLOCK, V7X_LANES), BF16),
            pltpu.VMEM((N_KV_HEADS, 2, BLOCK, V7X_LANES), BF16),
            pltpu.VMEM((tm, d), BF16),
            pltpu.VMEM((N_Q_HEADS, BLOCK, 2 * BLOCK), F32),
        ],
        compiler_params=_params(),
        name="attn_mixer",
    )(h, pre_g, post_g, w_qkv, b_qkv, w_o, b_o, bucket, cap, sinks, rel_bias)


def kernel(x, mix_pre_g, mix_post_g, ffn_pre_g, ffn_post_g, conv_w_in, conv_b_in, conv_dw_w, conv_dw_b, conv_ln_g, conv_ln_b, conv_w_out, conv_b_out, attn_w_qkv, attn_b_qkv, attn_w_o, attn_b_o, attn_sinks, rel_bias, ffn_w_gate_up, ffn_w_down):
    b, s, d = x.shape
    depth = mix_pre_g.shape[0]
    h = x.reshape(b * s, d)
    row = lambda v: v.reshape(1, -1)
    for i in range(depth):
        j = i // 2
        ffn_ops = [row(ffn_pre_g[i]), row(ffn_post_g[i]),
                   ffn_w_gate_up[i].astype(BF16), ffn_w_down[i].astype(BF16)]
        if i % 2 == 0:
            mixer_ops = [row(mix_pre_g[i]), row(mix_post_g[i]),
                         conv_w_in[j].astype(BF16), row(conv_b_in[j]), conv_dw_w[j],
                         row(conv_dw_b[j]), row(conv_ln_g[j]), row(conv_ln_b[j]),
                         conv_w_out[j].astype(BF16), row(conv_b_out[j])]
            h = _conv_layer_call(h, s, mixer_ops, ffn_ops)
        else:
            h = _attn_call(
                h, s, row(mix_pre_g[i]), row(mix_post_g[i]),
                attn_w_qkv[j].astype(BF16), row(attn_b_qkv[j]),
                attn_w_o[j].astype(BF16), row(attn_b_o[j]), attn_sinks[j], rel_bias)
            h = _ffn_call(h, *ffn_ops)
    return h.reshape(b, s, d)
```

```python
import functools
import math

import jax
import jax.numpy as jnp
import numpy as np
from jax import lax
from jax.experimental import pallas as pl
from jax.experimental.pallas import tpu as pltpu

D_MODEL = 1024
CONV_WIDTH = 31
HEAD_DIM = 64
N_Q_HEADS = D_MODEL // HEAD_DIM
N_KV_HEADS = 4
GROUP = N_Q_HEADS // N_KV_HEADS
WINDOW = 128
BLOCK = 128
NUM_BUCKETS = 32
MAX_DISTANCE = 128
D_FF = ((8 * D_MODEL // 3 + 255) // 256) * 256
EPS = 1e-6

V7X_LANES = 128
V7X_SUBLANES = 8
V7X_MXU_DIM = 256

TOKEN_TILE = 512
LAYER_TILE = 256
MXU_CHUNK = V7X_MXU_DIM
CONV_HALO = 32
CONV_SUBTILE = 64
VMEM_LIMIT = 56 * 1024 * 1024

F32 = jnp.float32
BF16 = jnp.bfloat16


def _rms(x, g):
    ms = jnp.mean(x * x, axis=-1, keepdims=True)
    return x * lax.rsqrt(ms + EPS) * g


def _sigmoid(x):
    return 1.0 / (1.0 + jnp.exp(-x))


def _const_spec(shape):
    zeros = (0,) * len(shape)
    return pl.BlockSpec(shape, lambda i: zeros, pipeline_mode=pl.Buffered(1))


def _layer_spec(stacked, layer):
    zeros = (0,) * (stacked.ndim - 1)
    return pl.BlockSpec((None,) + stacked.shape[1:], lambda i: (layer,) + zeros,
                        pipeline_mode=pl.Buffered(1))


def _tile_spec(tm, d):
    return pl.BlockSpec((tm, d), lambda i: (i, 0))


def _params(**flags):
    return pltpu.CompilerParams(
        dimension_semantics=("arbitrary",), vmem_limit_bytes=VMEM_LIMIT,
        flags=flags or None)


def _ffn_kernel(h_ref, pre_ref, post_ref, wgu_ref, wd_ref, o_ref):
    h = h_ref[...]
    xn = _rms(h, pre_ref[...]).astype(BF16)
    g = jnp.dot(xn, wgu_ref[:, :D_FF], preferred_element_type=F32)
    u = jnp.dot(xn, wgu_ref[:, D_FF:], preferred_element_type=F32)
    a = (g * _sigmoid(g) * u).astype(BF16)
    f = jnp.dot(a, wd_ref[...], preferred_element_type=F32)
    o_ref[...] = h + _rms(f, post_ref[...])


def _ffn_call(h, layer, pre_g, post_g, w_gu, w_d):
    t, d = h.shape
    tm = TOKEN_TILE
    return pl.pallas_call(
        _ffn_kernel,
        grid=(t // tm,),
        in_specs=[
            _tile_spec(tm, d),
            _const_spec((1, d)),
            _const_spec((1, d)),
            _layer_spec(w_gu, layer),
            _layer_spec(w_d, layer),
        ],
        out_specs=_tile_spec(tm, d),
        out_shape=jax.ShapeDtypeStruct((t, d), F32),
        compiler_params=_params(),
        name="ffn",
    )(h, pre_g, post_g, w_gu, w_d)


def _zero_row(token):
    bits = pltpu.bitcast(token, jnp.uint32)
    bits = lax.shift_right_logical(lax.shift_right_logical(bits, jnp.uint32(16)), jnp.uint32(16))
    return pltpu.bitcast(bits, F32)[0:1, :]


class _Slots:
    def __init__(self):
        self._vec_dep = None
        self._mm_dep = None

    def run(self, vector_tasks, matmul_tasks):
        vec_tok = None
        for task in vector_tasks:
            tok = task(self._mm_dep)
            if tok is not None:
                vec_tok = tok if vec_tok is None else vec_tok + tok
        mm_tok = None
        for task in matmul_tasks:
            tok = task(self._vec_dep)
            mm_tok = tok if tok is not None else mm_tok
        self._vec_dep = None if vec_tok is None else _zero_row(vec_tok)
        self._mm_dep = None if mm_tok is None else _zero_row(mm_tok)


def _conv_layer_kernel(tiles_per_seq, xg_ref, xm_ref, pre_ref, post_ref, win_ref, bin_ref,
                       dww_ref, dwb_ref, lng_ref, lnb_ref, wout_ref, bout_ref,
                       fpre_ref, fpost_ref, wgu_ref, wd_ref,
                       o_ref, xng_ref, xnf_ref, h1_ref, a_ref, graw_ref, uraw_ref,
                       f_ref, fo_ref, z_ref, gnext_ref, ext_ref, y_ref):
    tm, d = xg_ref.shape
    s = pl.program_id(0)
    c = MXU_CHUNK
    n_glu, n_up, n_out = d // c, D_FF // c, d // c

    @pl.when(s == 0)
    def _():
        xnf_ref[...] = jnp.zeros(xnf_ref.shape, BF16)
        h1_ref[...] = jnp.zeros(h1_ref.shape, F32)
        ext_ref[...] = jnp.zeros(ext_ref.shape, F32)

    @pl.when((s - 1) % tiles_per_seq == 0)
    def _():
        ext_ref[0:CONV_HALO, :] = jnp.zeros((CONV_HALO, d), F32)

    def lhs(ref, dep):
        xn = ref[...]
        if dep is None:
            return xn
        row = dep.astype(xn.dtype)
        return xn + jnp.concatenate([row] * (xn.shape[1] // V7X_LANES), axis=1)

    def token(x):
        return x[0:V7X_SUBLANES, 0:V7X_LANES]

    def rms_g(dep):
        xng_ref[...] = _rms(xg_ref[...], pre_ref[...]).astype(BF16)

    def glu_mm(j, dep):
        xn = lhs(xng_ref, dep)
        val = jnp.dot(xn, win_ref[:, j * c:(j + 1) * c], preferred_element_type=F32)
        graw_ref[j % 4, :, 0:c] = val
        graw_ref[j % 4, :, c:2 * c] = jnp.dot(
            xn, win_ref[:, d + j * c:d + (j + 1) * c], preferred_element_type=F32)
        return token(val)

    def glu_act(j, dep):
        val = graw_ref[j % 4, :, 0:c] + bin_ref[:, j * c:(j + 1) * c]
        gate = graw_ref[j % 4, :, c:2 * c] + bin_ref[:, d + j * c:d + (j + 1) * c]
        gnext_ref[:, j * c:(j + 1) * c] = val * _sigmoid(gate)

    off0 = CONV_HALO - (CONV_WIDTH - 1)
    ts = CONV_SUBTILE
    n_rows = ts + CONV_HALO

    def conv(c0, r0, dep):
        lanes = slice(c0, c0 + V7X_LANES)
        rows = ext_ref[r0:r0 + n_rows, lanes]
        bias = dwb_ref[:, lanes] if dep is None else dwb_ref[:, lanes] + dep
        acc = jnp.broadcast_to(bias, (ts, V7X_LANES))
        for r in range(V7X_SUBLANES):
            offs = [o for o in range(off0, off0 + CONV_WIDTH) if o % V7X_SUBLANES == r]
            xr = rows if r == 0 else pltpu.roll(rows, n_rows - r, axis=0)
            for o in offs:
                a = o - r
                acc = acc + xr[a:a + ts, :] * dww_ref[o - off0:o - off0 + 1, lanes]
        y_ref[r0:r0 + ts, lanes] = acc
        return sum(acc[i:i + V7X_SUBLANES] for i in range(0, ts, V7X_SUBLANES))

    convs = [functools.partial(conv, c0, r0)
             for c0 in range(0, d, V7X_LANES) for r0 in range(0, tm, ts)]

    def layer_norm_act(dep):
        y = y_ref[...]
        mu = jnp.mean(y, axis=-1, keepdims=True)
        yc = y - mu
        var = jnp.mean(yc * yc, axis=-1, keepdims=True)
        z = yc * lax.rsqrt(var + EPS) * lng_ref[...] + lnb_ref[...]
        z = z * _sigmoid(z)
        z_ref[...] = z.astype(BF16)
        return token(z)

    def out_mm(j, dep):
        w = 2 * c
        fo = jnp.dot(lhs(z_ref, dep), wout_ref[:, j * w:(j + 1) * w], preferred_element_type=F32)
        fo_ref[:, j * w:(j + 1) * w] = fo
        return token(fo)

    def refill_ext(dep):
        ext_ref[0:CONV_HALO, :] = ext_ref[tm:tm + CONV_HALO, :]
        ext_ref[CONV_HALO:CONV_HALO + tm, :] = gnext_ref[...]

    def mixer_finish(dep):
        h1 = xm_ref[...] + _rms(fo_ref[...] + bout_ref[...], post_ref[...])
        h1_ref[...] = h1
        xnf_ref[...] = _rms(h1, fpre_ref[...]).astype(BF16)

    def up_mm(j, dep):
        xn = lhs(xnf_ref, dep)
        g = jnp.dot(xn, wgu_ref[:, j * c:(j + 1) * c], preferred_element_type=F32)
        uraw_ref[j % 4, :, 0:c] = g
        uraw_ref[j % 4, :, c:2 * c] = jnp.dot(
            xn, wgu_ref[:, D_FF + j * c:D_FF + (j + 1) * c], preferred_element_type=F32)
        return token(g)

    def up_act(j, dep):
        g = uraw_ref[j % 4, :, 0:c]
        u = uraw_ref[j % 4, :, c:2 * c]
        a_ref[:, j * c:(j + 1) * c] = (g * _sigmoid(g) * u).astype(BF16)

    def down_mm(j, dep):
        w = 2 * c
        f = jnp.dot(lhs(a_ref, dep), wd_ref[:, j * w:(j + 1) * w], preferred_element_type=F32)
        f_ref[:, j * w:(j + 1) * w] = f
        return token(f)

    def ffn_finish(dep):
        o_ref[...] = h1_ref[...] + _rms(f_ref[...], fpost_ref[...])

    P = functools.partial
    slots = _Slots()
    per = -(-len(convs) // 8)
    cv = [convs[i * per:(i + 1) * per] for i in range(8)]
    assert n_up == 11 and n_glu == 4 and n_out == 4
    slots.run([rms_g] + cv[0], [P(up_mm, 0), P(up_mm, 1)])
    slots.run([P(up_act, 0), P(up_act, 1)] + cv[1], [P(up_mm, 2), P(up_mm, 3)])
    slots.run([P(up_act, 2), P(up_act, 3)] + cv[2], [P(up_mm, 4), P(up_mm, 5)])
    slots.run([P(up_act, 4), P(up_act, 5)] + cv[3], [P(up_mm, 6), P(up_mm, 7)])
    slots.run([P(up_act, 6), P(up_act, 7)] + cv[4], [P(up_mm, 8), P(up_mm, 9)])
    slots.run([P(up_act, 8), P(up_act, 9)] + cv[5], [P(up_mm, 10), P(glu_mm, 0)])
    slots.run([P(up_act, 10), P(glu_act, 0)] + cv[6], [P(glu_mm, 1), P(glu_mm, 2)])
    slots.run([P(glu_act, 1), P(glu_act, 2)] + cv[7], [P(glu_mm, 3), P(down_mm, 0)])
    slots.run([P(glu_act, 3), layer_norm_act], [P(down_mm, 1)])
    slots.run([refill_ext], [P(out_mm, 0), P(out_mm, 1)])
    slots.run([ffn_finish, mixer_finish], [])


def _conv_layer_call(h, seq_len, layer, mixer_ops, ffn_norms, w_gu, w_d):
    t, d = h.shape
    tm = LAYER_TILE
    assert seq_len % tm == 0 and t % seq_len == 0 and CONV_HALO >= CONV_WIDTH - 1
    n = t // tm
    ops = list(mixer_ops) + list(ffn_norms)
    return pl.pallas_call(
        functools.partial(_conv_layer_kernel, seq_len // tm),
        grid=(n + 2,),
        in_specs=[pl.BlockSpec((tm, d), lambda s: (jnp.minimum(s, n - 1), 0)),
                  pl.BlockSpec((tm, d), lambda s: (jnp.clip(s - 1, 0, n - 1), 0))]
        + [_const_spec(o.shape) for o in ops]
        + [_layer_spec(w_gu, layer), _layer_spec(w_d, layer)],
        out_specs=pl.BlockSpec((tm, d), lambda s: (jnp.clip(s - 2, 0, n - 1), 0)),
        out_shape=jax.ShapeDtypeStruct((t, d), F32),
        scratch_shapes=[
            pltpu.VMEM((tm, d), BF16),
            pltpu.VMEM((tm, d), BF16),
            pltpu.VMEM((tm, d), F32),
            pltpu.VMEM((tm, D_FF), BF16),
            pltpu.VMEM((4, tm, 2 * MXU_CHUNK), F32),
            pltpu.VMEM((4, tm, 2 * MXU_CHUNK), F32),
            pltpu.VMEM((tm, d), F32),
            pltpu.VMEM((tm, d), F32),
            pltpu.VMEM((tm, d), BF16),
            pltpu.VMEM((tm, d), F32),
            pltpu.VMEM((tm + CONV_HALO, d), F32),
            pltpu.VMEM((tm, d), F32),
        ],
        compiler_params=_params(),
        name="conv_layer",
    )(h, h, *ops, w_gu, w_d)


def _attn_kernel(tiles_per_seq, h_ref, pre_ref, post_ref, wqkv_ref, bqkv_ref,
                 wo_ref, bo_ref, bucket_ref, cap_ref, sink_ref, relb_ref,
                 o_ref, q_ref, kw_ref, vw_ref, kc_ref, vc_ref, ao_ref, bias_ref):
    tm, d = h_ref.shape
    nblk = tm // BLOCK
    half = V7X_LANES // 2
    i = pl.program_id(0)
    seq_start = i % tiles_per_seq == 0

    @pl.when(i == 0)
    def _():
        vw_ref[:, :, :, :, V7X_LANES:] = jnp.ones(
            (N_KV_HEADS, 2, nblk, 2 * BLOCK, V7X_LANES), BF16)
        bucket = bucket_ref[...]
        key_col = lax.broadcasted_iota(jnp.int32, bucket.shape, 1)

        def head_body(hh, carry):
            tile = jnp.zeros(bucket.shape, F32)
            for b in range(NUM_BUCKETS):
                tile = jnp.where(bucket == b, relb_ref[b, hh], tile)
            bias_ref[hh] = jnp.where(key_col == 0, sink_ref[hh], tile)
            return carry

        lax.fori_loop(0, N_Q_HEADS, head_body, 0)

    @pl.when(seq_start)
    def _():
        z = jnp.zeros((N_KV_HEADS, 2, BLOCK, V7X_LANES), BF16)
        kc_ref[...] = z
        vc_ref[...] = z

    h = h_ref[...]
    xn = _rms(h, pre_ref[...]).astype(BF16)

    scale = HEAD_DIM ** -0.5
    qkv = jnp.dot(xn, wqkv_ref[...], preferred_element_type=F32) + bqkv_ref[...]
    for kh in range(N_KV_HEADS):
        for pr in range(2):
            c0 = (2 * kh + pr) * V7X_LANES
            qp = (qkv[:, c0:c0 + V7X_LANES] * scale).astype(BF16)
            for j in range(nblk):
                q_ref[kh, j, pr * BLOCK:(pr + 1) * BLOCK, :] = qp[j * BLOCK:(j + 1) * BLOCK, :]

    lane = lax.broadcasted_iota(jnp.int32, (tm, V7X_LANES), 1)
    low = lane < half
    row = lax.broadcasted_iota(jnp.int32, (tm, V7X_LANES), 0)
    block_start = row % BLOCK == 0
    for which, win_ref, carry_ref in ((0, kw_ref, kc_ref), (1, vw_ref, vc_ref)):
        base = N_Q_HEADS * HEAD_DIM + which * N_KV_HEADS * HEAD_DIM
        for col in range(N_KV_HEADS // 2):
            c0 = base + col * V7X_LANES
            x = qkv[:, c0:c0 + V7X_LANES]
            xs = pltpu.roll(x, half, axis=1)
            zero = jnp.zeros_like(x)
            variants = {(2 * col, 0): jnp.where(low, x, zero),
                        (2 * col, 1): jnp.where(low, zero, xs),
                        (2 * col + 1, 0): jnp.where(low, xs, zero),
                        (2 * col + 1, 1): jnp.where(low, zero, x)}
            for (kh, var), val in variants.items():
                cur = val.astype(BF16)
                prv = jnp.where(block_start, zero, val).astype(BF16)
                win_ref[kh, var, 0, 0:BLOCK, 0:V7X_LANES] = carry_ref[kh, var]
                for j in range(nblk):
                    blk = slice(j * BLOCK, (j + 1) * BLOCK)
                    win_ref[kh, var, j, BLOCK:2 * BLOCK, 0:V7X_LANES] = cur[blk]
                    if j + 1 < nblk:
                        win_ref[kh, var, j + 1, 0:BLOCK, 0:V7X_LANES] = prv[blk]
                carry_ref[kh, var] = prv[(nblk - 1) * BLOCK:]

    lane2 = lax.broadcasted_iota(jnp.int32, (2 * BLOCK, V7X_LANES), 1)

    def block_body(j, carry):
        r0 = pl.multiple_of(j * BLOCK, BLOCK)
        first = jnp.logical_and(seq_start, j == 0).astype(jnp.int32)
        for kh in range(N_KV_HEADS):
            q2 = q_ref[kh, j]
            ls = []
            acc = None
            for eo in range(2):
                kb = kw_ref[kh, eo, j]
                vb = vw_ref[kh, eo, j]
                s = lax.dot_general(q2, kb, (((1,), (1,)), ((), ())),
                                    preferred_element_type=F32)
                h_top = 4 * kh + eo
                bias = jnp.concatenate([bias_ref[h_top], bias_ref[h_top + 2]], axis=0)
                s = jnp.minimum(s + bias, cap_ref[first])
                m = jnp.max(s, axis=-1, keepdims=True)
                p = jnp.exp(s - m).astype(BF16)
                pv = jnp.dot(p, vb, preferred_element_type=F32)
                ls.append(pv[:, V7X_LANES:])
                acc = pv[:, :V7X_LANES] if acc is None else acc + pv[:, :V7X_LANES]
            out = acc / jnp.where(lane2 < half, ls[0], ls[1])
            out = out.astype(BF16)
            c0 = kh * 2 * V7X_LANES
            ao_ref[pl.ds(r0, BLOCK), c0:c0 + V7X_LANES] = out[:BLOCK]
            ao_ref[pl.ds(r0, BLOCK), c0 + V7X_LANES:c0 + 2 * V7X_LANES] = out[BLOCK:]
        return carry

    lax.fori_loop(0, nblk, block_body, 0, unroll=True)

    f = jnp.dot(ao_ref[...], wo_ref[...], preferred_element_type=F32) + bo_ref[...]
    o_ref[...] = h + _rms(f, post_ref[...])


def _t5_causal_bucket_np(dist):
    dist = np.maximum(dist, 0)
    max_exact = NUM_BUCKETS // 2
    large = max_exact + (
        np.log(np.maximum(dist, 1).astype(np.float32) / np.float32(max_exact))
        / np.float32(math.log(MAX_DISTANCE / max_exact)) * np.float32(NUM_BUCKETS - max_exact)
    ).astype(np.int32)
    large = np.minimum(large, NUM_BUCKETS - 1)
    return np.where(dist < max_exact, dist, large)


def _attn_tables():
    q_loc = np.arange(BLOCK, dtype=np.int32)[:, None] + BLOCK
    s_loc = np.arange(2 * BLOCK, dtype=np.int32)[None, :]
    dist = q_loc - s_loc
    band = (dist >= 0) & (dist < WINDOW)
    bucket = _t5_causal_bucket_np(dist).astype(np.int32)
    first = band & (s_loc >= BLOCK)
    valid = np.stack([np.tile(band, (2, 1)), np.tile(first, (2, 1))])
    assert not valid[:, :, 0].any()
    valid[:, :, 0] = True
    cap = np.where(valid, np.inf, -np.inf).astype(np.float32)
    return jnp.asarray(bucket), jnp.asarray(cap)


def _attn_call(h, seq_len, pre_g, post_g, w_qkv, b_qkv, w_o, b_o, sinks, rel_bias):
    t, d = h.shape
    tm = TOKEN_TILE
    assert seq_len % tm == 0 and tm % BLOCK == 0
    nblk = tm // BLOCK
    bucket, cap = _attn_tables()
    return pl.pallas_call(
        functools.partial(_attn_kernel, seq_len // tm),
        grid=(t // tm,),
        in_specs=[
            _tile_spec(tm, d),
            _const_spec((1, d)),
            _const_spec((1, d)),
            _const_spec(w_qkv.shape),
            _const_spec((1, w_qkv.shape[1])),
            _const_spec(w_o.shape),
            _const_spec((1, d)),
            _const_spec(bucket.shape),
            _const_spec(cap.shape),
            pl.BlockSpec(memory_space=pltpu.SMEM),
            pl.BlockSpec(memory_space=pltpu.SMEM),
        ],
        out_specs=_tile_spec(tm, d),
        out_shape=jax.ShapeDtypeStruct((t, d), F32),
        scratch_shapes=[
            pltpu.VMEM((N_KV_HEADS, nblk, 2 * BLOCK, V7X_LANES), BF16),
            pltpu.VMEM((N_KV_HEADS, 2, nblk, 2 * BLOCK, V7X_LANES), BF16),
            pltpu.VMEM((N_KV_HEADS, 2, nblk, 2 * BLOCK, 2 * V7X_LANES), BF16),
            pltpu.VMEM((N_KV_HEADS, 2, BLOCK, V7X_LANES), BF16),
            pltpu.VMEM((N_KV_HEADS, 2, BLOCK, V7X_LANES), BF16),
            pltpu.VMEM((tm, d), BF16),
            pltpu.VMEM((N_Q_HEADS, BLOCK, 2 * BLOCK), F32),
        ],
        compiler_params=_params(),
        name="attn_mixer",
    )(h, pre_g, post_g, w_qkv, b_qkv, w_o, b_o, bucket, cap, sinks, rel_bias)


def kernel(x, mix_pre_g, mix_post_g, ffn_pre_g, ffn_post_g, conv_w_in, conv_b_in, conv_dw_w, conv_dw_b, conv_ln_g, conv_ln_b, conv_w_out, conv_b_out, attn_w_qkv, attn_b_qkv, attn_w_o, attn_b_o, attn_sinks, rel_bias, ffn_w_gate_up, ffn_w_down):
    b, s, d = x.shape
    depth = mix_pre_g.shape[0]
    h = x.reshape(b * s, d)
    row = lambda v: v.reshape(1, -1)
    w_gu = ffn_w_gate_up.astype(BF16)
    w_d = ffn_w_down.astype(BF16)
    for i in range(depth):
        j = i // 2
        ffn_norms = [row(ffn_pre_g[i]), row(ffn_post_g[i])]
        if i % 2 == 0:
            mixer_ops = [row(mix_pre_g[i]), row(mix_post_g[i]),
                         conv_w_in[j].astype(BF16), row(conv_b_in[j]), conv_dw_w[j],
                         row(conv_dw_b[j]), row(conv_ln_g[j]), row(conv_ln_b[j]),
                         conv_w_out[j].astype(BF16), row(conv_b_out[j])]
            h = _conv_layer_call(h, s, i, mixer_ops, ffn_norms, w_gu, w_d)
        else:
            h = _attn_call(
                h, s, row(mix_pre_g[i]), row(mix_post_g[i]),
                attn_w_qkv[j].astype(BF16), row(attn_b_qkv[j]),
                attn_w_o[j].astype(BF16), row(attn_b_o[j]), attn_sinks[j], rel_bias)
            h = _ffn_call(h, i, *ffn_norms, w_gu, w_d)
    return h.reshape(b, s, d)
```

```python
import functools
import math

import jax
import jax.numpy as jnp
import numpy as np
from jax import lax
from jax.experimental import pallas as pl
from jax.experimental.pallas import tpu as pltpu

D_MODEL = 1024
CONV_WIDTH = 31
HEAD_DIM = 64
N_Q_HEADS = D_MODEL // HEAD_DIM
N_KV_HEADS = 4
GROUP = N_Q_HEADS // N_KV_HEADS
WINDOW = 128
BLOCK = 128
NUM_BUCKETS = 32
MAX_DISTANCE = 128
D_FF = ((8 * D_MODEL // 3 + 255) // 256) * 256
EPS = 1e-6

V7X_LANES = 128
V7X_SUBLANES = 8
V7X_MXU_DIM = 256

TOKEN_TILE = 512
LAYER_TILE = 256
MXU_CHUNK = V7X_MXU_DIM
CONV_HALO = 32
CONV_SUBTILE = 64
VMEM_LIMIT = 56 * 1024 * 1024

F32 = jnp.float32
BF16 = jnp.bfloat16


def _rms(x, g):
    ms = jnp.mean(x * x, axis=-1, keepdims=True)
    return x * lax.rsqrt(ms + EPS) * g


def _sigmoid(x):
    return 1.0 / (1.0 + jnp.exp(-x))


def _const_spec(shape):
    zeros = (0,) * len(shape)
    return pl.BlockSpec(shape, lambda i: zeros, pipeline_mode=pl.Buffered(1))


def _layer_spec(stacked, layer):
    zeros = (0,) * (stacked.ndim - 1)
    return pl.BlockSpec((None,) + stacked.shape[1:], lambda i: (layer,) + zeros,
                        pipeline_mode=pl.Buffered(1))


def _tile_spec(tm, d):
    return pl.BlockSpec((tm, d), lambda i: (i, 0))


def _params(**flags):
    return pltpu.CompilerParams(
        dimension_semantics=("arbitrary",), vmem_limit_bytes=VMEM_LIMIT,
        flags=flags or None)


def _ffn_kernel(h_ref, pre_ref, post_ref, wgu_ref, wd_ref, o_ref):
    h = h_ref[...]
    xn = _rms(h, pre_ref[...]).astype(BF16)
    g = jnp.dot(xn, wgu_ref[:, :D_FF], preferred_element_type=F32)
    u = jnp.dot(xn, wgu_ref[:, D_FF:], preferred_element_type=F32)
    a = (g * _sigmoid(g) * u).astype(BF16)
    f = jnp.dot(a, wd_ref[...], preferred_element_type=F32)
    o_ref[...] = h + _rms(f, post_ref[...])


def _ffn_call(h, layer, pre_g, post_g, w_gu, w_d):
    t, d = h.shape
    tm = TOKEN_TILE
    return pl.pallas_call(
        _ffn_kernel,
        grid=(t // tm,),
        in_specs=[
            _tile_spec(tm, d),
            _const_spec((1, d)),
            _const_spec((1, d)),
            _layer_spec(w_gu, layer),
            _layer_spec(w_d, layer),
        ],
        out_specs=_tile_spec(tm, d),
        out_shape=jax.ShapeDtypeStruct((t, d), F32),
        compiler_params=_params(),
        name="ffn",
    )(h, pre_g, post_g, w_gu, w_d)


def _zero_row(token):
    bits = pltpu.bitcast(token, jnp.uint32)
    bits = lax.shift_right_logical(lax.shift_right_logical(bits, jnp.uint32(16)), jnp.uint32(16))
    return pltpu.bitcast(bits, F32)[0:1, :]


class _Slots:
    def __init__(self):
        self._vec_dep = None
        self._mm_dep = None

    def run(self, vector_tasks, matmul_tasks):
        vec_tok = None
        for task in vector_tasks:
            tok = task(self._mm_dep)
            if tok is not None:
                vec_tok = tok if vec_tok is None else vec_tok + tok
        mm_tok = None
        for task in matmul_tasks:
            tok = task(self._vec_dep)
            mm_tok = tok if tok is not None else mm_tok
        self._vec_dep = None if vec_tok is None else _zero_row(vec_tok)
        self._mm_dep = None if mm_tok is None else _zero_row(mm_tok)


def _conv_layer_kernel(tiles_per_seq, xg_ref, xm_ref, pre_ref, post_ref, win_ref, bin_ref,
                       dww_ref, dwb_ref, lng_ref, lnb_ref, wout_ref, bout_ref,
                       fpre_ref, fpost_ref, wgu_ref, wd_ref,
                       o_ref, xng_ref, xnf_ref, h1_ref, a_ref, graw_ref, uraw_ref,
                       f_ref, fo_ref, z_ref, gnext_ref, ext_ref, y_ref):
    tm, d = xg_ref.shape
    s = pl.program_id(0)
    c = MXU_CHUNK
    n_glu, n_up, n_out = d // c, D_FF // c, d // c

    @pl.when(s == 0)
    def _():
        xnf_ref[...] = jnp.zeros(xnf_ref.shape, BF16)
        h1_ref[...] = jnp.zeros(h1_ref.shape, F32)
        ext_ref[...] = jnp.zeros(ext_ref.shape, F32)

    @pl.when((s - 1) % tiles_per_seq == 0)
    def _():
        ext_ref[:, 0:CONV_HALO, :] = jnp.zeros((d // V7X_LANES, CONV_HALO, V7X_LANES), F32)

    def lhs(ref, dep):
        xn = ref[...]
        if dep is None:
            return xn
        row = dep.astype(xn.dtype)
        return xn + jnp.concatenate([row] * (xn.shape[1] // V7X_LANES), axis=1)

    def token(x):
        return x[0:V7X_SUBLANES, 0:V7X_LANES]

    def rms_g(dep):
        xng_ref[...] = _rms(xg_ref[...], pre_ref[...]).astype(BF16)

    def glu_mm(j, dep):
        xn = lhs(xng_ref, dep)
        val = jnp.dot(xn, win_ref[:, j * c:(j + 1) * c], preferred_element_type=F32)
        graw_ref[j % 4, :, 0:c] = val
        graw_ref[j % 4, :, c:2 * c] = jnp.dot(
            xn, win_ref[:, d + j * c:d + (j + 1) * c], preferred_element_type=F32)
        return token(val)

    def glu_act(j, dep):
        val = graw_ref[j % 4, :, 0:c] + bin_ref[:, j * c:(j + 1) * c]
        gate = graw_ref[j % 4, :, c:2 * c] + bin_ref[:, d + j * c:d + (j + 1) * c]
        gnext_ref[:, j * c:(j + 1) * c] = val * _sigmoid(gate)

    off0 = CONV_HALO - (CONV_WIDTH - 1)
    ts = CONV_SUBTILE
    n_rows = ts + CONV_HALO

    def conv(c0, r0, dep):
        lanes = slice(c0, c0 + V7X_LANES)
        cc = c0 // V7X_LANES
        bias = dwb_ref[:, lanes] if dep is None else dwb_ref[:, lanes] + dep
        acc = jnp.broadcast_to(bias, (ts, V7X_LANES))
        for o in range(off0, off0 + CONV_WIDTH):
            x = ext_ref[cc, r0 + o:r0 + o + ts, :]
            acc = acc + x * dww_ref[o - off0:o - off0 + 1, lanes]
        y_ref[r0:r0 + ts, lanes] = acc
        return sum(acc[i:i + V7X_SUBLANES] for i in range(0, ts, V7X_SUBLANES))

    convs = [functools.partial(conv, c0, r0)
             for c0 in range(0, d, V7X_LANES) for r0 in range(0, tm, ts)]

    def layer_norm_act(dep):
        y = y_ref[...]
        mu = jnp.mean(y, axis=-1, keepdims=True)
        yc = y - mu
        var = jnp.mean(yc * yc, axis=-1, keepdims=True)
        z = yc * lax.rsqrt(var + EPS) * lng_ref[...] + lnb_ref[...]
        z = z * _sigmoid(z)
        z_ref[...] = z.astype(BF16)
        return token(z)

    def out_mm(j, dep):
        w = 2 * c
        fo = jnp.dot(lhs(z_ref, dep), wout_ref[:, j * w:(j + 1) * w], preferred_element_type=F32)
        fo_ref[:, j * w:(j + 1) * w] = fo
        return token(fo)

    def refill_ext(dep):
        ext_ref[:, 0:CONV_HALO, :] = ext_ref[:, tm:tm + CONV_HALO, :]
        for cc in range(d // V7X_LANES):
            ext_ref[cc, CONV_HALO:CONV_HALO + tm, :] = gnext_ref[:, cc * V7X_LANES:(cc + 1) * V7X_LANES]

    def mixer_finish(dep):
        h1 = xm_ref[...] + _rms(fo_ref[...] + bout_ref[...], post_ref[...])
        h1_ref[...] = h1
        xnf_ref[...] = _rms(h1, fpre_ref[...]).astype(BF16)

    def up_mm(j, dep):
        xn = lhs(xnf_ref, dep)
        g = jnp.dot(xn, wgu_ref[:, j * c:(j + 1) * c], preferred_element_type=F32)
        uraw_ref[j % 4, :, 0:c] = g
        uraw_ref[j % 4, :, c:2 * c] = jnp.dot(
            xn, wgu_ref[:, D_FF + j * c:D_FF + (j + 1) * c], preferred_element_type=F32)
        return token(g)

    def up_act(j, dep):
        g = uraw_ref[j % 4, :, 0:c]
        u = uraw_ref[j % 4, :, c:2 * c]
        a_ref[:, j * c:(j + 1) * c] = (g * _sigmoid(g) * u).astype(BF16)

    def down_mm(j, dep):
        w = 2 * c
        f = jnp.dot(lhs(a_ref, dep), wd_ref[:, j * w:(j + 1) * w], preferred_element_type=F32)
        f_ref[:, j * w:(j + 1) * w] = f
        return token(f)

    def ffn_finish(dep):
        o_ref[...] = h1_ref[...] + _rms(f_ref[...], fpost_ref[...])

    P = functools.partial
    slots = _Slots()
    per = -(-len(convs) // 8)
    cv = [convs[i * per:(i + 1) * per] for i in range(8)]
    assert n_up == 11 and n_glu == 4 and n_out == 4
    slots.run([rms_g] + cv[0], [P(up_mm, 0), P(up_mm, 1)])
    slots.run([P(up_act, 0), P(up_act, 1)] + cv[1], [P(up_mm, 2), P(up_mm, 3)])
    slots.run([P(up_act, 2), P(up_act, 3)] + cv[2], [P(up_mm, 4), P(up_mm, 5)])
    slots.run([P(up_act, 4), P(up_act, 5)] + cv[3], [P(up_mm, 6), P(up_mm, 7)])
    slots.run([P(up_act, 6), P(up_act, 7)] + cv[4], [P(up_mm, 8), P(up_mm, 9)])
    slots.run([P(up_act, 8), P(up_act, 9)] + cv[5], [P(up_mm, 10), P(glu_mm, 0)])
    slots.run([P(up_act, 10), P(glu_act, 0)] + cv[6], [P(glu_mm, 1), P(glu_mm, 2)])
    slots.run([P(glu_act, 1), P(glu_act, 2)] + cv[7], [P(glu_mm, 3), P(down_mm, 0)])
    slots.run([P(glu_act, 3), layer_norm_act], [P(down_mm, 1)])
    slots.run([refill_ext], [P(out_mm, 0), P(out_mm, 1)])
    slots.run([ffn_finish, mixer_finish], [])


def _conv_layer_call(h, seq_len, layer, mixer_ops, ffn_norms, w_gu, w_d):
    t, d = h.shape
    tm = LAYER_TILE
    assert seq_len % tm == 0 and t % seq_len == 0 and CONV_HALO >= CONV_WIDTH - 1
    n = t // tm
    ops = list(mixer_ops) + list(ffn_norms)
    return pl.pallas_call(
        functools.partial(_conv_layer_kernel, seq_len // tm),
        grid=(n + 2,),
        in_specs=[pl.BlockSpec((tm, d), lambda s: (jnp.minimum(s, n - 1), 0)),
                  pl.BlockSpec((tm, d), lambda s: (jnp.clip(s - 1, 0, n - 1), 0))]
        + [_const_spec(o.shape) for o in ops]
        + [_layer_spec(w_gu, layer), _layer_spec(w_d, layer)],
        out_specs=pl.BlockSpec((tm, d), lambda s: (jnp.clip(s - 2, 0, n - 1), 0)),
        out_shape=jax.ShapeDtypeStruct((t, d), F32),
        scratch_shapes=[
            pltpu.VMEM((tm, d), BF16),
            pltpu.VMEM((tm, d), BF16),
            pltpu.VMEM((tm, d), F32),
            pltpu.VMEM((tm, D_FF), BF16),
            pltpu.VMEM((4, tm, 2 * MXU_CHUNK), F32),
            pltpu.VMEM((4, tm, 2 * MXU_CHUNK), F32),
            pltpu.VMEM((tm, d), F32),
            pltpu.VMEM((tm, d), F32),
            pltpu.VMEM((tm, d), BF16),
            pltpu.VMEM((tm, d), F32),
            pltpu.VMEM((d // V7X_LANES, tm + CONV_HALO, V7X_LANES), F32),
            pltpu.VMEM((tm, d), F32),
        ],
        compiler_params=_params(),
        name="conv_layer",
    )(h, h, *ops, w_gu, w_d)


def _attn_kernel(tiles_per_seq, h_ref, pre_ref, post_ref, wqkv_ref, bqkv_ref,
                 wo_ref, bo_ref, bucket_ref, cap_ref, sink_ref, relb_ref,
                 o_ref, q_ref, kw_ref, vw_ref, kc_ref, vc_ref, ao_ref, bias_ref):
    tm, d = h_ref.shape
    nblk = tm // BLOCK
    half = V7X_LANES // 2
    i = pl.program_id(0)
    seq_start = i % tiles_per_seq == 0

    @pl.when(i == 0)
    def _():
        vw_ref[:, :, :, :, V7X_LANES:] = jnp.ones(
            (N_KV_HEADS, 2, nblk, 2 * BLOCK, V7X_LANES), BF16)
        bucket = bucket_ref[...]
        key_col = lax.broadcasted_iota(jnp.int32, bucket.shape, 1)

        def head_body(hh, carry):
            tile = jnp.zeros(bucket.shape, F32)
            for b in range(NUM_BUCKETS):
                tile = jnp.where(bucket == b, relb_ref[b, hh], tile)
            bias_ref[hh] = jnp.where(key_col == 0, sink_ref[hh], tile)
            return carry

        lax.fori_loop(0, N_Q_HEADS, head_body, 0)

    @pl.when(seq_start)
    def _():
        z = jnp.zeros((N_KV_HEADS, 2, BLOCK, V7X_LANES), BF16)
        kc_ref[...] = z
        vc_ref[...] = z

    h = h_ref[...]
    xn = _rms(h, pre_ref[...]).astype(BF16)

    scale = HEAD_DIM ** -0.5
    qkv = jnp.dot(xn, wqkv_ref[...], preferred_element_type=F32) + bqkv_ref[...]
    for kh in range(N_KV_HEADS):
        for pr in range(2):
            c0 = (2 * kh + pr) * V7X_LANES
            qp = (qkv[:, c0:c0 + V7X_LANES] * scale).astype(BF16)
            for j in range(nblk):
                q_ref[kh, j, pr * BLOCK:(pr + 1) * BLOCK, :] = qp[j * BLOCK:(j + 1) * BLOCK, :]

    lane = lax.broadcasted_iota(jnp.int32, (tm, V7X_LANES), 1)
    low = lane < half
    row = lax.broadcasted_iota(jnp.int32, (tm, V7X_LANES), 0)
    block_start = row % BLOCK == 0
    for which, win_ref, carry_ref in ((0, kw_ref, kc_ref), (1, vw_ref, vc_ref)):
        base = N_Q_HEADS * HEAD_DIM + which * N_KV_HEADS * HEAD_DIM
        for col in range(N_KV_HEADS // 2):
            c0 = base + col * V7X_LANES
            x = qkv[:, c0:c0 + V7X_LANES]
            xs = pltpu.roll(x, half, axis=1)
            zero = jnp.zeros_like(x)
            variants = {(2 * col, 0): jnp.where(low, x, zero),
                        (2 * col, 1): jnp.where(low, zero, xs),
                        (2 * col + 1, 0): jnp.where(low, xs, zero),
                        (2 * col + 1, 1): jnp.where(low, zero, x)}
            for (kh, var), val in variants.items():
                cur = val.astype(BF16)
                prv = jnp.where(block_start, zero, val).astype(BF16)
                win_ref[kh, var, 0, 0:BLOCK, 0:V7X_LANES] = carry_ref[kh, var]
                for j in range(nblk):
                    blk = slice(j * BLOCK, (j + 1) * BLOCK)
                    win_ref[kh, var, j, BLOCK:2 * BLOCK, 0:V7X_LANES] = cur[blk]
                    if j + 1 < nblk:
                        win_ref[kh, var, j + 1, 0:BLOCK, 0:V7X_LANES] = prv[blk]
                carry_ref[kh, var] = prv[(nblk - 1) * BLOCK:]

    lane2 = lax.broadcasted_iota(jnp.int32, (2 * BLOCK, V7X_LANES), 1)

    def block_body(j, carry):
        r0 = pl.multiple_of(j * BLOCK, BLOCK)
        first = jnp.logical_and(seq_start, j == 0).astype(jnp.int32)
        for kh in range(N_KV_HEADS):
            q2 = q_ref[kh, j]
            ls = []
            acc = None
            for eo in range(2):
                kb = kw_ref[kh, eo, j]
                vb = vw_ref[kh, eo, j]
                s = lax.dot_general(q2, kb, (((1,), (1,)), ((), ())),
                                    preferred_element_type=F32)
                h_top = 4 * kh + eo
                bias = jnp.concatenate([bias_ref[h_top], bias_ref[h_top + 2]], axis=0)
                s = jnp.minimum(s + bias, cap_ref[first])
                m = jnp.max(s, axis=-1, keepdims=True)
                p = jnp.exp(s - m).astype(BF16)
                pv = jnp.dot(p, vb, preferred_element_type=F32)
                ls.append(pv[:, V7X_LANES:])
                acc = pv[:, :V7X_LANES] if acc is None else acc + pv[:, :V7X_LANES]
            out = acc / jnp.where(lane2 < half, ls[0], ls[1])
            out = out.astype(BF16)
            c0 = kh * 2 * V7X_LANES
            ao_ref[pl.ds(r0, BLOCK), c0:c0 + V7X_LANES] = out[:BLOCK]
            ao_ref[pl.ds(r0, BLOCK), c0 + V7X_LANES:c0 + 2 * V7X_LANES] = out[BLOCK:]
        return carry

    lax.fori_loop(0, nblk, block_body, 0, unroll=True)

    f = jnp.dot(ao_ref[...], wo_ref[...], preferred_element_type=F32) + bo_ref[...]
    o_ref[...] = h + _rms(f, post_ref[...])


def _t5_causal_bucket_np(dist):
    dist = np.maximum(dist, 0)
    max_exact = NUM_BUCKETS // 2
    large = max_exact + (
        np.log(np.maximum(dist, 1).astype(np.float32) / np.float32(max_exact))
        / np.float32(math.log(MAX_DISTANCE / max_exact)) * np.float32(NUM_BUCKETS - max_exact)
    ).astype(np.int32)
    large = np.minimum(large, NUM_BUCKETS - 1)
    return np.where(dist < max_exact, dist, large)


def _attn_tables():
    q_loc = np.arange(BLOCK, dtype=np.int32)[:, None] + BLOCK
    s_loc = np.arange(2 * BLOCK, dtype=np.int32)[None, :]
    dist = q_loc - s_loc
    band = (dist >= 0) & (dist < WINDOW)
    bucket = _t5_causal_bucket_np(dist).astype(np.int32)
    first = band & (s_loc >= BLOCK)
    valid = np.stack([np.tile(band, (2, 1)), np.tile(first, (2, 1))])
    assert not valid[:, :, 0].any()
    valid[:, :, 0] = True
    cap = np.where(valid, np.inf, -np.inf).astype(np.float32)
    return jnp.asarray(bucket), jnp.asarray(cap)


def _attn_call(h, seq_len, pre_g, post_g, w_qkv, b_qkv, w_o, b_o, sinks, rel_bias):
    t, d = h.shape
    tm = TOKEN_TILE
    assert seq_len % tm == 0 and tm % BLOCK == 0
    nblk = tm // BLOCK
    bucket, cap = _attn_tables()
    return pl.pallas_call(
        functools.partial(_attn_kernel, seq_len // tm),
        grid=(t // tm,),
        in_specs=[
            _tile_spec(tm, d),
            _const_spec((1, d)),
            _const_spec((1, d)),
            _const_spec(w_qkv.shape),
            _const_spec((1, w_qkv.shape[1])),
            _const_spec(w_o.shape),
            _const_spec((1, d)),
            _const_spec(bucket.shape),
            _const_spec(cap.shape),
            pl.BlockSpec(memory_space=pltpu.SMEM),
            pl.BlockSpec(memory_space=pltpu.SMEM),
        ],
        out_specs=_tile_spec(tm, d),
        out_shape=jax.ShapeDtypeStruct((t, d), F32),
        scratch_shapes=[
            pltpu.VMEM((N_KV_HEADS, nblk, 2 * BLOCK, V7X_LANES), BF16),
            pltpu.VMEM((N_KV_HEADS, 2, nblk, 2 * BLOCK, V7X_LANES), BF16),
            pltpu.VMEM((N_KV_HEADS, 2, nblk, 2 * BLOCK, 2 * V7X_LANES), BF16),
            pltpu.VMEM((N_KV_HEADS, 2, BLOCK, V7X_LANES), BF16),
            pltpu.VMEM((N_KV_HEADS, 2, BLOCK, V7X_LANES), BF16),
            pltpu.VMEM((tm, d), BF16),
            pltpu.VMEM((N_Q_HEADS, BLOCK, 2 * BLOCK), F32),
        ],
        compiler_params=_params(),
        name="attn_mixer",
    )(h, pre_g, post_g, w_qkv, b_qkv, w_o, b_o, bucket, cap, sinks, rel_bias)


def kernel(x, mix_pre_g, mix_post_g, ffn_pre_g, ffn_post_g, conv_w_in, conv_b_in, conv_dw_w, conv_dw_b, conv_ln_g, conv_ln_b, conv_w_out, conv_b_out, attn_w_qkv, attn_b_qkv, attn_w_o, attn_b_o, attn_sinks, rel_bias, ffn_w_gate_up, ffn_w_down):
    b, s, d = x.shape
    depth = mix_pre_g.shape[0]
    h = x.reshape(b * s, d)
    row = lambda v: v.reshape(1, -1)
    w_gu = ffn_w_gate_up.astype(BF16)
    w_d = ffn_w_down.astype(BF16)
    for i in range(depth):
        j = i // 2
        ffn_norms = [row(ffn_pre_g[i]), row(ffn_post_g[i])]
        if i % 2 == 0:
            mixer_ops = [row(mix_pre_g[i]), row(mix_post_g[i]),
                         conv_w_in[j].astype(BF16), row(conv_b_in[j]), conv_dw_w[j],
                         row(conv_dw_b[j]), row(conv_ln_g[j]), row(conv_ln_b[j]),
                         conv_w_out[j].astype(BF16), row(conv_b_out[j])]
            h = _conv_layer_call(h, s, i, mixer_ops, ffn_norms, w_gu, w_d)
        else:
            h = _attn_call(
                h, s, row(mix_pre_g[i]), row(mix_post_g[i]),
                attn_w_qkv[j].astype(BF16), row(attn_b_qkv[j]),
                attn_w_o[j].astype(BF16), row(attn_b_o[j]), attn_sinks[j], rel_bias)
            h = _ffn_call(h, i, *ffn_norms, w_gu, w_d)
    return h.reshape(b, s, d)
```

```python
import functools
import math

import jax
import jax.numpy as jnp
import numpy as np
from jax import lax
from jax.experimental import pallas as pl
from jax.experimental.pallas import tpu as pltpu

D_MODEL = 1024
CONV_WIDTH = 31
HEAD_DIM = 64
N_Q_HEADS = D_MODEL // HEAD_DIM
N_KV_HEADS = 4
GROUP = N_Q_HEADS // N_KV_HEADS
WINDOW = 128
BLOCK = 128
NUM_BUCKETS = 32
MAX_DISTANCE = 128
D_FF = ((8 * D_MODEL // 3 + 255) // 256) * 256
EPS = 1e-6

V7X_LANES = 128
V7X_SUBLANES = 8

TOKEN_TILE = 512
CONV_HALO = 32
CONV_SUBTILE = 64
VMEM_LIMIT = 56 * 1024 * 1024

F32 = jnp.float32
BF16 = jnp.bfloat16


def _rms(x, g):
    ms = jnp.mean(x * x, axis=-1, keepdims=True)
    return x * lax.rsqrt(ms + EPS) * g


def _sigmoid(x):
    return 1.0 / (1.0 + jnp.exp(-x))


def _const_spec(shape):
    zeros = (0,) * len(shape)
    return pl.BlockSpec(shape, lambda i: zeros, pipeline_mode=pl.Buffered(1))


def _layer_spec(stacked, layer):
    zeros = (0,) * (stacked.ndim - 1)
    return pl.BlockSpec((None,) + stacked.shape[1:], lambda i: (layer,) + zeros,
                        pipeline_mode=pl.Buffered(1))


def _tile_spec(tm, d):
    return pl.BlockSpec((tm, d), lambda i: (i, 0))


def _params():
    return pltpu.CompilerParams(
        dimension_semantics=("arbitrary",), vmem_limit_bytes=VMEM_LIMIT)


def _ffn_kernel(h_ref, pre_ref, post_ref, wgu_ref, wd_ref, o_ref):
    h = h_ref[...]
    xn = _rms(h, pre_ref[...]).astype(BF16)
    g = jnp.dot(xn, wgu_ref[:, :D_FF], preferred_element_type=F32)
    u = jnp.dot(xn, wgu_ref[:, D_FF:], preferred_element_type=F32)
    a = (g * _sigmoid(g) * u).astype(BF16)
    f = jnp.dot(a, wd_ref[...], preferred_element_type=F32)
    o_ref[...] = h + _rms(f, post_ref[...])


def _ffn_call(h, layer, pre_g, post_g, w_gu, w_d):
    t, d = h.shape
    tm = TOKEN_TILE
    return pl.pallas_call(
        _ffn_kernel,
        grid=(t // tm,),
        in_specs=[
            _tile_spec(tm, d),
            _const_spec((1, d)),
            _const_spec((1, d)),
            _layer_spec(w_gu, layer),
            _layer_spec(w_d, layer),
        ],
        out_specs=_tile_spec(tm, d),
        out_shape=jax.ShapeDtypeStruct((t, d), F32),
        compiler_params=_params(),
        name="ffn",
    )(h, pre_g, post_g, w_gu, w_d)


def _conv_kernel(tiles_per_seq, h_ref, pre_ref, post_ref, win_ref, bin_ref,
                 dww_ref, dwb_ref, lng_ref, lnb_ref, wout_ref, bout_ref,
                 o_ref, ext_ref, y_ref):
    tm, d = h_ref.shape
    n_lane = d // V7X_LANES
    i = pl.program_id(0)

    @pl.when(i % tiles_per_seq == 0)
    def _():
        ext_ref[:, 0:CONV_HALO, :] = jnp.zeros((n_lane, CONV_HALO, V7X_LANES), F32)

    h = h_ref[...]
    xn = _rms(h, pre_ref[...]).astype(BF16)
    val = jnp.dot(xn, win_ref[:, :d], preferred_element_type=F32) + bin_ref[:, :d]
    gate = jnp.dot(xn, win_ref[:, d:], preferred_element_type=F32) + bin_ref[:, d:]
    glu = val * _sigmoid(gate)
    for cc in range(n_lane):
        ext_ref[cc, CONV_HALO:CONV_HALO + tm, :] = glu[:, cc * V7X_LANES:(cc + 1) * V7X_LANES]

    off0 = CONV_HALO - (CONV_WIDTH - 1)
    ts = CONV_SUBTILE

    def chunk_body(cc, carry):
        for r0 in range(0, tm, ts):
            acc = jnp.broadcast_to(dwb_ref[cc], (ts, V7X_LANES))
            for k in range(CONV_WIDTH):
                acc = acc + ext_ref[cc, r0 + off0 + k:r0 + off0 + k + ts, :] * dww_ref[cc, k:k + 1, :]
            y_ref[cc, r0:r0 + ts, :] = acc
        return carry

    lax.fori_loop(0, n_lane, chunk_body, 0)

    ext_ref[:, 0:CONV_HALO, :] = ext_ref[:, tm:tm + CONV_HALO, :]

    y = jnp.concatenate([y_ref[cc] for cc in range(n_lane)], axis=1)
    mu = jnp.mean(y, axis=-1, keepdims=True)
    yc = y - mu
    var = jnp.mean(yc * yc, axis=-1, keepdims=True)
    z = yc * lax.rsqrt(var + EPS) * lng_ref[...] + lnb_ref[...]
    z = (z * _sigmoid(z)).astype(BF16)
    f = jnp.dot(z, wout_ref[...], preferred_element_type=F32) + bout_ref[...]
    o_ref[...] = h + _rms(f, post_ref[...])


def _conv_call(h, seq_len, pre_g, post_g, w_in, b_in, dw_w, dw_b, ln_g, ln_b,
               w_out, b_out):
    t, d = h.shape
    tm = TOKEN_TILE
    assert seq_len % tm == 0 and CONV_HALO >= CONV_WIDTH - 1
    n_lane = d // V7X_LANES
    dw_w = dw_w.reshape(CONV_WIDTH, n_lane, V7X_LANES).transpose(1, 0, 2)
    dw_b = dw_b.reshape(n_lane, 1, V7X_LANES)
    return pl.pallas_call(
        functools.partial(_conv_kernel, seq_len // tm),
        grid=(t // tm,),
        in_specs=[
            _tile_spec(tm, d),
            _const_spec((1, d)),
            _const_spec((1, d)),
            _const_spec(w_in.shape),
            _const_spec((1, 2 * d)),
            _const_spec(dw_w.shape),
            _const_spec(dw_b.shape),
            _const_spec((1, d)),
            _const_spec((1, d)),
            _const_spec(w_out.shape),
            _const_spec((1, d)),
        ],
        out_specs=_tile_spec(tm, d),
        out_shape=jax.ShapeDtypeStruct((t, d), F32),
        scratch_shapes=[
            pltpu.VMEM((n_lane, tm + CONV_HALO, V7X_LANES), F32),
            pltpu.VMEM((n_lane, tm, V7X_LANES), F32),
        ],
        compiler_params=_params(),
        name="conv_mixer",
    )(h, pre_g, post_g, w_in, b_in, dw_w, dw_b, ln_g, ln_b, w_out, b_out)


def _attn_kernel(tiles_per_seq, h_ref, pre_ref, post_ref, wqkv_ref, bqkv_ref,
                 wo_ref, bo_ref, bucket_ref, cap_ref, sink_ref, relb_ref,
                 o_ref, q_ref, kw_ref, vw_ref, kc_ref, vc_ref, ao_ref, bias_ref):
    tm, d = h_ref.shape
    nblk = tm // BLOCK
    half = V7X_LANES // 2
    i = pl.program_id(0)
    seq_start = i % tiles_per_seq == 0

    @pl.when(i == 0)
    def _():
        vw_ref[:, :, :, :, V7X_LANES:] = jnp.ones(
            (N_KV_HEADS, 2, nblk, 2 * BLOCK, V7X_LANES), BF16)
        bucket = bucket_ref[...]
        key_col = lax.broadcasted_iota(jnp.int32, bucket.shape, 1)

        def head_body(hh, carry):
            tile = jnp.zeros(bucket.shape, F32)
            for b in range(NUM_BUCKETS):
                tile = jnp.where(bucket == b, relb_ref[b, hh], tile)
            bias_ref[hh] = jnp.where(key_col == 0, sink_ref[hh], tile)
            return carry

        lax.fori_loop(0, N_Q_HEADS, head_body, 0)

    @pl.when(seq_start)
    def _():
        z = jnp.zeros((N_KV_HEADS, 2, BLOCK, V7X_LANES), BF16)
        kc_ref[...] = z
        vc_ref[...] = z

    h = h_ref[...]
    xn = _rms(h, pre_ref[...]).astype(BF16)

    scale = HEAD_DIM ** -0.5
    qkv = jnp.dot(xn, wqkv_ref[...], preferred_element_type=F32) + bqkv_ref[...]
    for kh in range(N_KV_HEADS):
        for pr in range(2):
            c0 = (2 * kh + pr) * V7X_LANES
            qp = (qkv[:, c0:c0 + V7X_LANES] * scale).astype(BF16)
            for j in range(nblk):
                q_ref[kh, j, pr * BLOCK:(pr + 1) * BLOCK, :] = qp[j * BLOCK:(j + 1) * BLOCK, :]

    lane = lax.broadcasted_iota(jnp.int32, (tm, V7X_LANES), 1)
    low = lane < half
    row = lax.broadcasted_iota(jnp.int32, (tm, V7X_LANES), 0)
    block_start = row % BLOCK == 0
    for which, win_ref, carry_ref in ((0, kw_ref, kc_ref), (1, vw_ref, vc_ref)):
        base = N_Q_HEADS * HEAD_DIM + which * N_KV_HEADS * HEAD_DIM
        for col in range(N_KV_HEADS // 2):
            c0 = base + col * V7X_LANES
            x = qkv[:, c0:c0 + V7X_LANES]
            xs = pltpu.roll(x, half, axis=1)
            zero = jnp.zeros_like(x)
            variants = {(2 * col, 0): jnp.where(low, x, zero),
                        (2 * col, 1): jnp.where(low, zero, xs),
                        (2 * col + 1, 0): jnp.where(low, xs, zero),
                        (2 * col + 1, 1): jnp.where(low, zero, x)}
            for (kh, var), val in variants.items():
                cur = val.astype(BF16)
                prv = jnp.where(block_start, zero, val).astype(BF16)
                win_ref[kh, var, 0, 0:BLOCK, 0:V7X_LANES] = carry_ref[kh, var]
                for j in range(nblk):
                    blk = slice(j * BLOCK, (j + 1) * BLOCK)
                    win_ref[kh, var, j, BLOCK:2 * BLOCK, 0:V7X_LANES] = cur[blk]
                    if j + 1 < nblk:
                        win_ref[kh, var, j + 1, 0:BLOCK, 0:V7X_LANES] = prv[blk]
                carry_ref[kh, var] = prv[(nblk - 1) * BLOCK:]

    lane2 = lax.broadcasted_iota(jnp.int32, (2 * BLOCK, V7X_LANES), 1)

    def block_body(j, carry):
        r0 = pl.multiple_of(j * BLOCK, BLOCK)
        first = jnp.logical_and(seq_start, j == 0).astype(jnp.int32)
        for kh in range(N_KV_HEADS):
            q2 = q_ref[kh, j]
            ls = []
            acc = None
            for eo in range(2):
                kb = kw_ref[kh, eo, j]
                vb = vw_ref[kh, eo, j]
                s = lax.dot_general(q2, kb, (((1,), (1,)), ((), ())),
                                    preferred_element_type=F32)
                h_top = 4 * kh + eo
                bias = jnp.concatenate([bias_ref[h_top], bias_ref[h_top + 2]], axis=0)
                s = jnp.minimum(s + bias, cap_ref[first])
                m = jnp.max(s, axis=-1, keepdims=True)
                p = jnp.exp(s - m).astype(BF16)
                pv = jnp.dot(p, vb, preferred_element_type=F32)
                ls.append(pv[:, V7X_LANES:])
                acc = pv[:, :V7X_LANES] if acc is None else acc + pv[:, :V7X_LANES]
            out = acc / jnp.where(lane2 < half, ls[0], ls[1])
            out = out.astype(BF16)
            c0 = kh * 2 * V7X_LANES
            ao_ref[pl.ds(r0, BLOCK), c0:c0 + V7X_LANES] = out[:BLOCK]
            ao_ref[pl.ds(r0, BLOCK), c0 + V7X_LANES:c0 + 2 * V7X_LANES] = out[BLOCK:]
        return carry

    lax.fori_loop(0, nblk, block_body, 0, unroll=True)

    f = jnp.dot(ao_ref[...], wo_ref[...], preferred_element_type=F32) + bo_ref[...]
    o_ref[...] = h + _rms(f, post_ref[...])


def _t5_causal_bucket_np(dist):
    dist = np.maximum(dist, 0)
    max_exact = NUM_BUCKETS // 2
    large = max_exact + (
        np.log(np.maximum(dist, 1).astype(np.float32) / np.float32(max_exact))
        / np.float32(math.log(MAX_DISTANCE / max_exact)) * np.float32(NUM_BUCKETS - max_exact)
    ).astype(np.int32)
    large = np.minimum(large, NUM_BUCKETS - 1)
    return np.where(dist < max_exact, dist, large)


def _attn_tables():
    q_loc = np.arange(BLOCK, dtype=np.int32)[:, None] + BLOCK
    s_loc = np.arange(2 * BLOCK, dtype=np.int32)[None, :]
    dist = q_loc - s_loc
    band = (dist >= 0) & (dist < WINDOW)
    bucket = _t5_causal_bucket_np(dist).astype(np.int32)
    first = band & (s_loc >= BLOCK)
    valid = np.stack([np.tile(band, (2, 1)), np.tile(first, (2, 1))])
    assert not valid[:, :, 0].any()
    valid[:, :, 0] = True
    cap = np.where(valid, np.inf, -np.inf).astype(np.float32)
    return jnp.asarray(bucket), jnp.asarray(cap)


def _attn_call(h, seq_len, pre_g, post_g, w_qkv, b_qkv, w_o, b_o, sinks, rel_bias):
    t, d = h.shape
    tm = TOKEN_TILE
    assert seq_len % tm == 0 and tm % BLOCK == 0
    nblk = tm // BLOCK
    bucket, cap = _attn_tables()
    return pl.pallas_call(
        functools.partial(_attn_kernel, seq_len // tm),
        grid=(t // tm,),
        in_specs=[
            _tile_spec(tm, d),
            _const_spec((1, d)),
            _const_spec((1, d)),
            _const_spec(w_qkv.shape),
            _const_spec((1, w_qkv.shape[1])),
            _const_spec(w_o.shape),
            _const_spec((1, d)),
            _const_spec(bucket.shape),
            _const_spec(cap.shape),
            pl.BlockSpec(memory_space=pltpu.SMEM),
            pl.BlockSpec(memory_space=pltpu.SMEM),
        ],
        out_specs=_tile_spec(tm, d),
        out_shape=jax.ShapeDtypeStruct((t, d), F32),
        scratch_shapes=[
            pltpu.VMEM((N_KV_HEADS, nblk, 2 * BLOCK, V7X_LANES), BF16),
            pltpu.VMEM((N_KV_HEADS, 2, nblk, 2 * BLOCK, V7X_LANES), BF16),
            pltpu.VMEM((N_KV_HEADS, 2, nblk, 2 * BLOCK, 2 * V7X_LANES), BF16),
            pltpu.VMEM((N_KV_HEADS, 2, BLOCK, V7X_LANES), BF16),
            pltpu.VMEM((N_KV_HEADS, 2, BLOCK, V7X_LANES), BF16),
            pltpu.VMEM((tm, d), BF16),
            pltpu.VMEM((N_Q_HEADS, BLOCK, 2 * BLOCK), F32),
        ],
        compiler_params=_params(),
        name="attn_mixer",
    )(h, pre_g, post_g, w_qkv, b_qkv, w_o, b_o, bucket, cap, sinks, rel_bias)


def kernel(x, mix_pre_g, mix_post_g, ffn_pre_g, ffn_post_g, conv_w_in, conv_b_in, conv_dw_w, conv_dw_b, conv_ln_g, conv_ln_b, conv_w_out, conv_b_out, attn_w_qkv, attn_b_qkv, attn_w_o, attn_b_o, attn_sinks, rel_bias, ffn_w_gate_up, ffn_w_down):
    b, s, d = x.shape
    depth = mix_pre_g.shape[0]
    h = x.reshape(b * s, d)
    row = lambda v: v.reshape(1, -1)
    w_gu = ffn_w_gate_up.astype(BF16)
    w_d = ffn_w_down.astype(BF16)
    for i in range(depth):
        j = i // 2
        if i % 2 == 0:
            h = _conv_call(
                h, s, row(mix_pre_g[i]), row(mix_post_g[i]),
                conv_w_in[j].astype(BF16), row(conv_b_in[j]), conv_dw_w[j],
                row(conv_dw_b[j]), row(conv_ln_g[j]), row(conv_ln_b[j]),
                conv_w_out[j].astype(BF16), row(conv_b_out[j]))
        else:
            h = _attn_call(
                h, s, row(mix_pre_g[i]), row(mix_post_g[i]),
                attn_w_qkv[j].astype(BF16), row(attn_b_qkv[j]),
                attn_w_o[j].astype(BF16), row(attn_b_o[j]), attn_sinks[j], rel_bias)
        h = _ffn_call(h, i, row(ffn_pre_g[i]), row(ffn_post_g[i]), w_gu, w_d)
    return h.reshape(b, s, d)
```

```python
import functools
import math

import jax
import jax.numpy as jnp
import numpy as np
from jax import lax
from jax.experimental import pallas as pl
from jax.experimental.pallas import tpu as pltpu

D_MODEL = 1024
CONV_WIDTH = 31
HEAD_DIM = 64
N_Q_HEADS = D_MODEL // HEAD_DIM
N_KV_HEADS = 4
GROUP = N_Q_HEADS // N_KV_HEADS
WINDOW = 128
BLOCK = 128
NUM_BUCKETS = 32
MAX_DISTANCE = 128
D_FF = ((8 * D_MODEL // 3 + 255) // 256) * 256
EPS = 1e-6

V7X_LANES = 128
V7X_SUBLANES = 8

TOKEN_TILE = 512
CONV_HALO = 32
CONV_SUBTILE = 64
VMEM_LIMIT = 56 * 1024 * 1024

F32 = jnp.float32
BF16 = jnp.bfloat16


def _rms(x, g):
    ms = jnp.mean(x * x, axis=-1, keepdims=True)
    return x * lax.rsqrt(ms + EPS) * g


def _sigmoid(x):
    return 1.0 / (1.0 + jnp.exp(-x))


def _const_spec(shape):
    zeros = (0,) * len(shape)
    return pl.BlockSpec(shape, lambda i: zeros, pipeline_mode=pl.Buffered(1))


def _layer_spec(stacked, layer):
    zeros = (0,) * (stacked.ndim - 1)
    return pl.BlockSpec((None,) + stacked.shape[1:], lambda i: (layer,) + zeros,
                        pipeline_mode=pl.Buffered(1))


def _tile_spec(tm, d):
    return pl.BlockSpec((tm, d), lambda i: (i, 0))


def _params():
    return pltpu.CompilerParams(
        dimension_semantics=("arbitrary",), vmem_limit_bytes=VMEM_LIMIT)


def _ffn_kernel(h_ref, pre_ref, post_ref, wgu_ref, wd_ref, o_ref):
    h = h_ref[...]
    xn = _rms(h, pre_ref[...]).astype(BF16)
    g = jnp.dot(xn, wgu_ref[:, :D_FF], preferred_element_type=F32)
    u = jnp.dot(xn, wgu_ref[:, D_FF:], preferred_element_type=F32)
    a = (g * _sigmoid(g) * u).astype(BF16)
    f = jnp.dot(a, wd_ref[...], preferred_element_type=F32)
    o_ref[...] = h + _rms(f, post_ref[...])


def _ffn_call(h, layer, pre_g, post_g, w_gu, w_d):
    t, d = h.shape
    tm = TOKEN_TILE
    return pl.pallas_call(
        _ffn_kernel,
        grid=(t // tm,),
        in_specs=[
            _tile_spec(tm, d),
            _const_spec((1, d)),
            _const_spec((1, d)),
            _layer_spec(w_gu, layer),
            _layer_spec(w_d, layer),
        ],
        out_specs=_tile_spec(tm, d),
        out_shape=jax.ShapeDtypeStruct((t, d), F32),
        compiler_params=_params(),
        name="ffn",
    )(h, pre_g, post_g, w_gu, w_d)


def _conv_kernel(tiles_per_seq, h_ref, pre_ref, post_ref, win_ref, bin_ref,
                 dww_ref, dwb_ref, lng_ref, lnb_ref, wout_ref, bout_ref,
                 o_ref, ext_ref, y_ref):
    tm, d = h_ref.shape
    n_lane = d // V7X_LANES
    i = pl.program_id(0)

    @pl.when(i % tiles_per_seq == 0)
    def _():
        ext_ref[:, 0:CONV_HALO, :] = jnp.zeros((n_lane, CONV_HALO, V7X_LANES), F32)

    h = h_ref[...]
    xn = _rms(h, pre_ref[...]).astype(BF16)
    off0 = CONV_HALO - (CONV_WIDTH - 1)
    ts = CONV_SUBTILE

    def chunk_body(cc, carry):
        for r0 in range(0, tm, ts):
            acc = jnp.broadcast_to(dwb_ref[cc], (ts, V7X_LANES))
            for k in range(CONV_WIDTH):
                acc = acc + ext_ref[cc, r0 + off0 + k:r0 + off0 + k + ts, :] * dww_ref[cc, k:k + 1, :]
            y_ref[cc, r0:r0 + ts, :] = acc
        return carry

    wide = 2 * V7X_LANES
    for c0 in range(0, d, wide):
        val = jnp.dot(xn, win_ref[:, c0:c0 + wide], preferred_element_type=F32) + bin_ref[:, c0:c0 + wide]
        gate = jnp.dot(xn, win_ref[:, d + c0:d + c0 + wide],
                       preferred_element_type=F32) + bin_ref[:, d + c0:d + c0 + wide]
        glu = val * _sigmoid(gate)
        for cc in range(c0 // V7X_LANES, (c0 + wide) // V7X_LANES):
            lo = cc * V7X_LANES - c0
            ext_ref[cc, CONV_HALO:CONV_HALO + tm, :] = glu[:, lo:lo + V7X_LANES]
            chunk_body(cc, 0)

    ext_ref[:, 0:CONV_HALO, :] = ext_ref[:, tm:tm + CONV_HALO, :]

    y = jnp.concatenate([y_ref[cc] for cc in range(n_lane)], axis=1)
    mu = jnp.mean(y, axis=-1, keepdims=True)
    yc = y - mu
    var = jnp.mean(yc * yc, axis=-1, keepdims=True)
    z = yc * lax.rsqrt(var + EPS) * lng_ref[...] + lnb_ref[...]
    z = (z * _sigmoid(z)).astype(BF16)
    f = jnp.dot(z, wout_ref[...], preferred_element_type=F32) + bout_ref[...]
    o_ref[...] = h + _rms(f, post_ref[...])


def _conv_call(h, seq_len, pre_g, post_g, w_in, b_in, dw_w, dw_b, ln_g, ln_b,
               w_out, b_out):
    t, d = h.shape
    tm = TOKEN_TILE
    assert seq_len % tm == 0 and CONV_HALO >= CONV_WIDTH - 1
    n_lane = d // V7X_LANES
    dw_w = dw_w.reshape(CONV_WIDTH, n_lane, V7X_LANES).transpose(1, 0, 2)
    dw_b = dw_b.reshape(n_lane, 1, V7X_LANES)
    return pl.pallas_call(
        functools.partial(_conv_kernel, seq_len // tm),
        grid=(t // tm,),
        in_specs=[
            _tile_spec(tm, d),
            _const_spec((1, d)),
            _const_spec((1, d)),
            _const_spec(w_in.shape),
            _const_spec((1, 2 * d)),
            _const_spec(dw_w.shape),
            _const_spec(dw_b.shape),
            _const_spec((1, d)),
            _const_spec((1, d)),
            _const_spec(w_out.shape),
            _const_spec((1, d)),
        ],
        out_specs=_tile_spec(tm, d),
        out_shape=jax.ShapeDtypeStruct((t, d), F32),
        scratch_shapes=[
            pltpu.VMEM((n_lane, tm + CONV_HALO, V7X_LANES), F32),
            pltpu.VMEM((n_lane, tm, V7X_LANES), F32),
        ],
        compiler_params=_params(),
        name="conv_mixer",
    )(h, pre_g, post_g, w_in, b_in, dw_w, dw_b, ln_g, ln_b, w_out, b_out)


def _attn_kernel(tiles_per_seq, h_ref, pre_ref, post_ref, wqkv_ref, bqkv_ref,
                 wo_ref, bo_ref, bucket_ref, cap_ref, sink_ref, relb_ref,
                 o_ref, q_ref, kw_ref, vw_ref, kc_ref, vc_ref, ao_ref, bias_ref):
    tm, d = h_ref.shape
    nblk = tm // BLOCK
    half = V7X_LANES // 2
    i = pl.program_id(0)
    seq_start = i % tiles_per_seq == 0

    @pl.when(i == 0)
    def _():
        vw_ref[:, :, :, :, V7X_LANES:] = jnp.ones(
            (N_KV_HEADS, 2, nblk, 2 * BLOCK, V7X_LANES), BF16)
        bucket = bucket_ref[...]
        key_col = lax.broadcasted_iota(jnp.int32, bucket.shape, 1)

        def head_body(hh, carry):
            tile = jnp.zeros(bucket.shape, F32)
            for b in range(NUM_BUCKETS):
                tile = jnp.where(bucket == b, relb_ref[b, hh], tile)
            bias_ref[hh] = jnp.where(key_col == 0, sink_ref[hh], tile)
            return carry

        lax.fori_loop(0, N_Q_HEADS, head_body, 0)

    @pl.when(seq_start)
    def _():
        z = jnp.zeros((N_KV_HEADS, 2, BLOCK, V7X_LANES), BF16)
        kc_ref[...] = z
        vc_ref[...] = z

    h = h_ref[...]
    xn = _rms(h, pre_ref[...]).astype(BF16)

    scale = HEAD_DIM ** -0.5
    qkv = jnp.dot(xn, wqkv_ref[...], preferred_element_type=F32) + bqkv_ref[...]
    for kh in range(N_KV_HEADS):
        for pr in range(2):
            c0 = (2 * kh + pr) * V7X_LANES
            qp = (qkv[:, c0:c0 + V7X_LANES] * scale).astype(BF16)
            for j in range(nblk):
                q_ref[kh, j, pr * BLOCK:(pr + 1) * BLOCK, :] = qp[j * BLOCK:(j + 1) * BLOCK, :]

    lane = lax.broadcasted_iota(jnp.int32, (tm, V7X_LANES), 1)
    low = lane < half
    row = lax.broadcasted_iota(jnp.int32, (tm, V7X_LANES), 0)
    block_start = row % BLOCK == 0
    for which, win_ref, carry_ref in ((0, kw_ref, kc_ref), (1, vw_ref, vc_ref)):
        base = N_Q_HEADS * HEAD_DIM + which * N_KV_HEADS * HEAD_DIM
        for col in range(N_KV_HEADS // 2):
            c0 = base + col * V7X_LANES
            x = qkv[:, c0:c0 + V7X_LANES]
            xs = pltpu.roll(x, half, axis=1)
            zero = jnp.zeros_like(x)
            variants = {(2 * col, 0): jnp.where(low, x, zero),
                        (2 * col, 1): jnp.where(low, zero, xs),
                        (2 * col + 1, 0): jnp.where(low, xs, zero),
                        (2 * col + 1, 1): jnp.where(low, zero, x)}
            for (kh, var), val in variants.items():
                cur = val.astype(BF16)
                prv = jnp.where(block_start, zero, val).astype(BF16)
                win_ref[kh, var, 0, 0:BLOCK, 0:V7X_LANES] = carry_ref[kh, var]
                for j in range(nblk):
                    blk = slice(j * BLOCK, (j + 1) * BLOCK)
                    win_ref[kh, var, j, BLOCK:2 * BLOCK, 0:V7X_LANES] = cur[blk]
                    if j + 1 < nblk:
                        win_ref[kh, var, j + 1, 0:BLOCK, 0:V7X_LANES] = prv[blk]
                carry_ref[kh, var] = prv[(nblk - 1) * BLOCK:]

    lane2 = lax.broadcasted_iota(jnp.int32, (2 * BLOCK, V7X_LANES), 1)

    def block_body(j, carry):
        r0 = pl.multiple_of(j * BLOCK, BLOCK)
        first = jnp.logical_and(seq_start, j == 0).astype(jnp.int32)
        for kh in range(N_KV_HEADS):
            q2 = q_ref[kh, j]
            ls = []
            acc = None
            for eo in range(2):
                kb = kw_ref[kh, eo, j]
                vb = vw_ref[kh, eo, j]
                s = lax.dot_general(q2, kb, (((1,), (1,)), ((), ())),
                                    preferred_element_type=F32)
                h_top = 4 * kh + eo
                bias = jnp.concatenate([bias_ref[h_top], bias_ref[h_top + 2]], axis=0)
                s = jnp.minimum(s + bias, cap_ref[first])
                m = jnp.max(s, axis=-1, keepdims=True)
                p = jnp.exp(s - m).astype(BF16)
                pv = jnp.dot(p, vb, preferred_element_type=F32)
                ls.append(pv[:, V7X_LANES:])
                acc = pv[:, :V7X_LANES] if acc is None else acc + pv[:, :V7X_LANES]
            out = acc / jnp.where(lane2 < half, ls[0], ls[1])
            out = out.astype(BF16)
            c0 = kh * 2 * V7X_LANES
            ao_ref[pl.ds(r0, BLOCK), c0:c0 + V7X_LANES] = out[:BLOCK]
            ao_ref[pl.ds(r0, BLOCK), c0 + V7X_LANES:c0 + 2 * V7X_LANES] = out[BLOCK:]
        return carry

    lax.fori_loop(0, nblk, block_body, 0, unroll=True)

    f = jnp.dot(ao_ref[...], wo_ref[...], preferred_element_type=F32) + bo_ref[...]
    o_ref[...] = h + _rms(f, post_ref[...])


def _t5_causal_bucket_np(dist):
    dist = np.maximum(dist, 0)
    max_exact = NUM_BUCKETS // 2
    large = max_exact + (
        np.log(np.maximum(dist, 1).astype(np.float32) / np.float32(max_exact))
        / np.float32(math.log(MAX_DISTANCE / max_exact)) * np.float32(NUM_BUCKETS - max_exact)
    ).astype(np.int32)
    large = np.minimum(large, NUM_BUCKETS - 1)
    return np.where(dist < max_exact, dist, large)


def _attn_tables():
    q_loc = np.arange(BLOCK, dtype=np.int32)[:, None] + BLOCK
    s_loc = np.arange(2 * BLOCK, dtype=np.int32)[None, :]
    dist = q_loc - s_loc
    band = (dist >= 0) & (dist < WINDOW)
    bucket = _t5_causal_bucket_np(dist).astype(np.int32)
    first = band & (s_loc >= BLOCK)
    valid = np.stack([np.tile(band, (2, 1)), np.tile(first, (2, 1))])
    assert not valid[:, :, 0].any()
    valid[:, :, 0] = True
    cap = np.where(valid, np.inf, -np.inf).astype(np.float32)
    return jnp.asarray(bucket), jnp.asarray(cap)


def _attn_call(h, seq_len, pre_g, post_g, w_qkv, b_qkv, w_o, b_o, sinks, rel_bias):
    t, d = h.shape
    tm = TOKEN_TILE
    assert seq_len % tm == 0 and tm % BLOCK == 0
    nblk = tm // BLOCK
    bucket, cap = _attn_tables()
    return pl.pallas_call(
        functools.partial(_attn_kernel, seq_len // tm),
        grid=(t // tm,),
        in_specs=[
            _tile_spec(tm, d),
            _const_spec((1, d)),
            _const_spec((1, d)),
            _const_spec(w_qkv.shape),
            _const_spec((1, w_qkv.shape[1])),
            _const_spec(w_o.shape),
            _const_spec((1, d)),
            _const_spec(bucket.shape),
            _const_spec(cap.shape),
            pl.BlockSpec(memory_space=pltpu.SMEM),
            pl.BlockSpec(memory_space=pltpu.SMEM),
        ],
        out_specs=_tile_spec(tm, d),
        out_shape=jax.ShapeDtypeStruct((t, d), F32),
        scratch_shapes=[
            pltpu.VMEM((N_KV_HEADS, nblk, 2 * BLOCK, V7X_LANES), BF16),
            pltpu.VMEM((N_KV_HEADS, 2, nblk, 2 * BLOCK, V7X_LANES), BF16),
            pltpu.VMEM((N_KV_HEADS, 2, nblk, 2 * BLOCK, 2 * V7X_LANES), BF16),
            pltpu.VMEM((N_KV_HEADS, 2, BLOCK, V7X_LANES), BF16),
            pltpu.VMEM((N_KV_HEADS, 2, BLOCK, V7X_LANES), BF16),
            pltpu.VMEM((tm, d), BF16),
            pltpu.VMEM((N_Q_HEADS, BLOCK, 2 * BLOCK), F32),
        ],
        compiler_params=_params(),
        name="attn_mixer",
    )(h, pre_g, post_g, w_qkv, b_qkv, w_o, b_o, bucket, cap, sinks, rel_bias)


def kernel(x, mix_pre_g, mix_post_g, ffn_pre_g, ffn_post_g, conv_w_in, conv_b_in, conv_dw_w, conv_dw_b, conv_ln_g, conv_ln_b, conv_w_out, conv_b_out, attn_w_qkv, attn_b_qkv, attn_w_o, attn_b_o, attn_sinks, rel_bias, ffn_w_gate_up, ffn_w_down):
    b, s, d = x.shape
    depth = mix_pre_g.shape[0]
    h = x.reshape(b * s, d)
    row = lambda v: v.reshape(1, -1)
    w_gu = ffn_w_gate_up.astype(BF16)
    w_d = ffn_w_down.astype(BF16)
    for i in range(depth):
        j = i // 2
        if i % 2 == 0:
            h = _conv_call(
                h, s, row(mix_pre_g[i]), row(mix_post_g[i]),
                conv_w_in[j].astype(BF16), row(conv_b_in[j]), conv_dw_w[j],
                row(conv_dw_b[j]), row(conv_ln_g[j]), row(conv_ln_b[j]),
                conv_w_out[j].astype(BF16), row(conv_b_out[j]))
        else:
            h = _attn_call(
                h, s, row(mix_pre_g[i]), row(mix_post_g[i]),
                attn_w_qkv[j].astype(BF16), row(attn_b_qkv[j]),
                attn_w_o[j].astype(BF16), row(attn_b_o[j]), attn_sinks[j], rel_bias)
        h = _ffn_call(h, i, row(ffn_pre_g[i]), row(ffn_post_g[i]), w_gu, w_d)
    return h.reshape(b, s, d)
```

```python
import functools
import math

import jax
import jax.numpy as jnp
import numpy as np
from jax import lax
from jax.experimental import pallas as pl
from jax.experimental.pallas import tpu as pltpu

D_MODEL = 1024
CONV_WIDTH = 31
HEAD_DIM = 64
N_Q_HEADS = D_MODEL // HEAD_DIM
N_KV_HEADS = 4
GROUP = N_Q_HEADS // N_KV_HEADS
WINDOW = 128
BLOCK = 128
NUM_BUCKETS = 32
MAX_DISTANCE = 128
D_FF = ((8 * D_MODEL // 3 + 255) // 256) * 256
EPS = 1e-6

V7X_LANES = 128
V7X_SUBLANES = 8

TOKEN_TILE = 512
CONV_HALO = 32
CONV_SUBTILE = 64
VMEM_LIMIT = 56 * 1024 * 1024

F32 = jnp.float32
BF16 = jnp.bfloat16


def _rms(x, g):
    ms = jnp.mean(x * x, axis=-1, keepdims=True)
    return x * lax.rsqrt(ms + EPS) * g


def _sigmoid(x):
    return 1.0 / (1.0 + jnp.exp(-x))


def _const_spec(shape):
    zeros = (0,) * len(shape)
    return pl.BlockSpec(shape, lambda i: zeros, pipeline_mode=pl.Buffered(1))


def _layer_spec(stacked, layer):
    zeros = (0,) * (stacked.ndim - 1)
    return pl.BlockSpec((None,) + stacked.shape[1:], lambda i: (layer,) + zeros,
                        pipeline_mode=pl.Buffered(1))


def _tile_spec(tm, d):
    return pl.BlockSpec((tm, d), lambda i: (i, 0))


def _params():
    return pltpu.CompilerParams(
        dimension_semantics=("arbitrary",), vmem_limit_bytes=VMEM_LIMIT)


def _ffn_kernel(h_ref, pre_ref, post_ref, wgu_ref, wd_ref, o_ref):
    tm = h_ref.shape[0]
    for r0 in range(0, tm, tm // 2):
        rows = slice(r0, r0 + tm // 2)
        h = h_ref[rows, :]
        xn = _rms(h, pre_ref[...]).astype(BF16)
        g = jnp.dot(xn, wgu_ref[:, :D_FF], preferred_element_type=F32)
        u = jnp.dot(xn, wgu_ref[:, D_FF:], preferred_element_type=F32)
        a = (g * _sigmoid(g) * u).astype(BF16)
        f = jnp.dot(a, wd_ref[...], preferred_element_type=F32)
        o_ref[rows, :] = h + _rms(f, post_ref[...])


def _ffn_call(h, layer, pre_g, post_g, w_gu, w_d):
    t, d = h.shape
    tm = TOKEN_TILE
    return pl.pallas_call(
        _ffn_kernel,
        grid=(t // tm,),
        in_specs=[
            _tile_spec(tm, d),
            _const_spec((1, d)),
            _const_spec((1, d)),
            _layer_spec(w_gu, layer),
            _layer_spec(w_d, layer),
        ],
        out_specs=_tile_spec(tm, d),
        out_shape=jax.ShapeDtypeStruct((t, d), F32),
        compiler_params=_params(),
        name="ffn",
    )(h, pre_g, post_g, w_gu, w_d)


def _conv_kernel(tiles_per_seq, h_ref, pre_ref, post_ref, win_ref, bin_ref,
                 dww_ref, dwb_ref, lng_ref, lnb_ref, wout_ref, bout_ref,
                 o_ref, ext_ref, y_ref):
    tm, d = h_ref.shape
    n_lane = d // V7X_LANES
    i = pl.program_id(0)

    @pl.when(i % tiles_per_seq == 0)
    def _():
        ext_ref[:, 0:CONV_HALO, :] = jnp.zeros((n_lane, CONV_HALO, V7X_LANES), F32)

    h = h_ref[...]
    xn = _rms(h, pre_ref[...]).astype(BF16)
    off0 = CONV_HALO - (CONV_WIDTH - 1)
    ts = CONV_SUBTILE

    def chunk_body(cc, carry):
        for r0 in range(0, tm, ts):
            acc = jnp.broadcast_to(dwb_ref[cc], (ts, V7X_LANES))
            for k in range(CONV_WIDTH):
                acc = acc + ext_ref[cc, r0 + off0 + k:r0 + off0 + k + ts, :] * dww_ref[cc, k:k + 1, :]
            y_ref[cc, r0:r0 + ts, :] = acc
        return carry

    wide = 2 * V7X_LANES
    for c0 in range(0, d, wide):
        val = jnp.dot(xn, win_ref[:, c0:c0 + wide], preferred_element_type=F32) + bin_ref[:, c0:c0 + wide]
        gate = jnp.dot(xn, win_ref[:, d + c0:d + c0 + wide],
                       preferred_element_type=F32) + bin_ref[:, d + c0:d + c0 + wide]
        glu = val * _sigmoid(gate)
        for cc in range(c0 // V7X_LANES, (c0 + wide) // V7X_LANES):
            lo = cc * V7X_LANES - c0
            ext_ref[cc, CONV_HALO:CONV_HALO + tm, :] = glu[:, lo:lo + V7X_LANES]
            chunk_body(cc, 0)

    ext_ref[:, 0:CONV_HALO, :] = ext_ref[:, tm:tm + CONV_HALO, :]

    y = jnp.concatenate([y_ref[cc] for cc in range(n_lane)], axis=1)
    mu = jnp.mean(y, axis=-1, keepdims=True)
    yc = y - mu
    var = jnp.mean(yc * yc, axis=-1, keepdims=True)
    z = yc * lax.rsqrt(var + EPS) * lng_ref[...] + lnb_ref[...]
    z = (z * _sigmoid(z)).astype(BF16)
    f = jnp.dot(z, wout_ref[...], preferred_element_type=F32) + bout_ref[...]
    o_ref[...] = h + _rms(f, post_ref[...])


def _conv_call(h, seq_len, pre_g, post_g, w_in, b_in, dw_w, dw_b, ln_g, ln_b,
               w_out, b_out):
    t, d = h.shape
    tm = TOKEN_TILE
    assert seq_len % tm == 0 and CONV_HALO >= CONV_WIDTH - 1
    n_lane = d // V7X_LANES
    dw_w = dw_w.reshape(CONV_WIDTH, n_lane, V7X_LANES).transpose(1, 0, 2)
    dw_b = dw_b.reshape(n_lane, 1, V7X_LANES)
    return pl.pallas_call(
        functools.partial(_conv_kernel, seq_len // tm),
        grid=(t // tm,),
        in_specs=[
            _tile_spec(tm, d),
            _const_spec((1, d)),
            _const_spec((1, d)),
            _const_spec(w_in.shape),
            _const_spec((1, 2 * d)),
            _const_spec(dw_w.shape),
            _const_spec(dw_b.shape),
            _const_spec((1, d)),
            _const_spec((1, d)),
            _const_spec(w_out.shape),
            _const_spec((1, d)),
        ],
        out_specs=_tile_spec(tm, d),
        out_shape=jax.ShapeDtypeStruct((t, d), F32),
        scratch_shapes=[
            pltpu.VMEM((n_lane, tm + CONV_HALO, V7X_LANES), F32),
            pltpu.VMEM((n_lane, tm, V7X_LANES), F32),
        ],
        compiler_params=_params(),
        name="conv_mixer",
    )(h, pre_g, post_g, w_in, b_in, dw_w, dw_b, ln_g, ln_b, w_out, b_out)


def _attn_kernel(tiles_per_seq, h_ref, pre_ref, post_ref, wqkv_ref, bqkv_ref,
                 wo_ref, bo_ref, bucket_ref, cap_ref, sink_ref, relb_ref,
                 o_ref, q_ref, kw_ref, vw_ref, kc_ref, vc_ref, ao_ref, bias_ref):
    tm, d = h_ref.shape
    nblk = tm // BLOCK
    half = V7X_LANES // 2
    i = pl.program_id(0)
    seq_start = i % tiles_per_seq == 0

    @pl.when(i == 0)
    def _():
        vw_ref[:, :, :, :, V7X_LANES:] = jnp.ones(
            (N_KV_HEADS, 2, nblk, 2 * BLOCK, V7X_LANES), BF16)
        bucket = bucket_ref[...]
        key_col = lax.broadcasted_iota(jnp.int32, bucket.shape, 1)

        def head_body(hh, carry):
            tile = jnp.zeros(bucket.shape, F32)
            for b in range(NUM_BUCKETS):
                tile = jnp.where(bucket == b, relb_ref[b, hh], tile)
            bias_ref[hh] = jnp.where(key_col == 0, sink_ref[hh], tile)
            return carry

        lax.fori_loop(0, N_Q_HEADS, head_body, 0)

    @pl.when(seq_start)
    def _():
        z = jnp.zeros((N_KV_HEADS, 2, BLOCK, V7X_LANES), BF16)
        kc_ref[...] = z
        vc_ref[...] = z

    h = h_ref[...]
    xn = _rms(h, pre_ref[...]).astype(BF16)

    scale = HEAD_DIM ** -0.5
    qkv = jnp.dot(xn, wqkv_ref[...], preferred_element_type=F32) + bqkv_ref[...]
    for kh in range(N_KV_HEADS):
        for pr in range(2):
            c0 = (2 * kh + pr) * V7X_LANES
            qp = (qkv[:, c0:c0 + V7X_LANES] * scale).astype(BF16)
            for j in range(nblk):
                q_ref[kh, j, pr * BLOCK:(pr + 1) * BLOCK, :] = qp[j * BLOCK:(j + 1) * BLOCK, :]

    lane = lax.broadcasted_iota(jnp.int32, (tm, V7X_LANES), 1)
    low = lane < half
    row = lax.broadcasted_iota(jnp.int32, (tm, V7X_LANES), 0)
    block_start = row % BLOCK == 0
    for which, win_ref, carry_ref in ((0, kw_ref, kc_ref), (1, vw_ref, vc_ref)):
        base = N_Q_HEADS * HEAD_DIM + which * N_KV_HEADS * HEAD_DIM
        for col in range(N_KV_HEADS // 2):
            c0 = base + col * V7X_LANES
            x = qkv[:, c0:c0 + V7X_LANES]
            xs = pltpu.roll(x, half, axis=1)
            zero = jnp.zeros_like(x)
            variants = {(2 * col, 0): jnp.where(low, x, zero),
                        (2 * col, 1): jnp.where(low, zero, xs),
                        (2 * col + 1, 0): jnp.where(low, xs, zero),
                        (2 * col + 1, 1): jnp.where(low, zero, x)}
            for (kh, var), val in variants.items():
                cur = val.astype(BF16)
                prv = jnp.where(block_start, zero, val).astype(BF16)
                win_ref[kh, var, 0, 0:BLOCK, 0:V7X_LANES] = carry_ref[kh, var]
                for j in range(nblk):
                    blk = slice(j * BLOCK, (j + 1) * BLOCK)
                    win_ref[kh, var, j, BLOCK:2 * BLOCK, 0:V7X_LANES] = cur[blk]
                    if j + 1 < nblk:
                        win_ref[kh, var, j + 1, 0:BLOCK, 0:V7X_LANES] = prv[blk]
                carry_ref[kh, var] = prv[(nblk - 1) * BLOCK:]

    lane2 = lax.broadcasted_iota(jnp.int32, (2 * BLOCK, V7X_LANES), 1)

    def block_body(j, carry):
        r0 = pl.multiple_of(j * BLOCK, BLOCK)
        first = jnp.logical_and(seq_start, j == 0).astype(jnp.int32)
        for kh in range(N_KV_HEADS):
            q2 = q_ref[kh, j]
            ls = []
            acc = None
            for eo in range(2):
                kb = kw_ref[kh, eo, j]
                vb = vw_ref[kh, eo, j]
                s = lax.dot_general(q2, kb, (((1,), (1,)), ((), ())),
                                    preferred_element_type=F32)
                h_top = 4 * kh + eo
                bias = jnp.concatenate([bias_ref[h_top], bias_ref[h_top + 2]], axis=0)
                s = jnp.minimum(s + bias, cap_ref[first])
                m = jnp.max(s, axis=-1, keepdims=True)
                p = jnp.exp(s - m).astype(BF16)
                pv = jnp.dot(p, vb, preferred_element_type=F32)
                ls.append(pv[:, V7X_LANES:])
                acc = pv[:, :V7X_LANES] if acc is None else acc + pv[:, :V7X_LANES]
            out = acc / jnp.where(lane2 < half, ls[0], ls[1])
            out = out.astype(BF16)
            c0 = kh * 2 * V7X_LANES
            ao_ref[pl.ds(r0, BLOCK), c0:c0 + V7X_LANES] = out[:BLOCK]
            ao_ref[pl.ds(r0, BLOCK), c0 + V7X_LANES:c0 + 2 * V7X_LANES] = out[BLOCK:]
        return carry

    lax.fori_loop(0, nblk, block_body, 0, unroll=True)

    f = jnp.dot(ao_ref[...], wo_ref[...], preferred_element_type=F32) + bo_ref[...]
    o_ref[...] = h + _rms(f, post_ref[...])


def _t5_causal_bucket_np(dist):
    dist = np.maximum(dist, 0)
    max_exact = NUM_BUCKETS // 2
    large = max_exact + (
        np.log(np.maximum(dist, 1).astype(np.float32) / np.float32(max_exact))
        / np.float32(math.log(MAX_DISTANCE / max_exact)) * np.float32(NUM_BUCKETS - max_exact)
    ).astype(np.int32)
    large = np.minimum(large, NUM_BUCKETS - 1)
    return np.where(dist < max_exact, dist, large)


def _attn_tables():
    q_loc = np.arange(BLOCK, dtype=np.int32)[:, None] + BLOCK
    s_loc = np.arange(2 * BLOCK, dtype=np.int32)[None, :]
    dist = q_loc - s_loc
    band = (dist >= 0) & (dist < WINDOW)
    bucket = _t5_causal_bucket_np(dist).astype(np.int32)
    first = band & (s_loc >= BLOCK)
    valid = np.stack([np.tile(band, (2, 1)), np.tile(first, (2, 1))])
    assert not valid[:, :, 0].any()
    valid[:, :, 0] = True
    cap = np.where(valid, np.inf, -np.inf).astype(np.float32)
    return jnp.asarray(bucket), jnp.asarray(cap)


def _attn_call(h, seq_len, pre_g, post_g, w_qkv, b_qkv, w_o, b_o, sinks, rel_bias):
    t, d = h.shape
    tm = TOKEN_TILE
    assert seq_len % tm == 0 and tm % BLOCK == 0
    nblk = tm // BLOCK
    bucket, cap = _attn_tables()
    return pl.pallas_call(
        functools.partial(_attn_kernel, seq_len // tm),
        grid=(t // tm,),
        in_specs=[
            _tile_spec(tm, d),
            _const_spec((1, d)),
            _const_spec((1, d)),
            _const_spec(w_qkv.shape),
            _const_spec((1, w_qkv.shape[1])),
            _const_spec(w_o.shape),
            _const_spec((1, d)),
            _const_spec(bucket.shape),
            _const_spec(cap.shape),
            pl.BlockSpec(memory_space=pltpu.SMEM),
            pl.BlockSpec(memory_space=pltpu.SMEM),
        ],
        out_specs=_tile_spec(tm, d),
        out_shape=jax.ShapeDtypeStruct((t, d), F32),
        scratch_shapes=[
            pltpu.VMEM((N_KV_HEADS, nblk, 2 * BLOCK, V7X_LANES), BF16),
            pltpu.VMEM((N_KV_HEADS, 2, nblk, 2 * BLOCK, V7X_LANES), BF16),
            pltpu.VMEM((N_KV_HEADS, 2, nblk, 2 * BLOCK, 2 * V7X_LANES), BF16),
            pltpu.VMEM((N_KV_HEADS, 2, BLOCK, V7X_LANES), BF16),
            pltpu.VMEM((N_KV_HEADS, 2, BLOCK, V7X_LANES), BF16),
            pltpu.VMEM((tm, d), BF16),
            pltpu.VMEM((N_Q_HEADS, BLOCK, 2 * BLOCK), F32),
        ],
        compiler_params=_params(),
        name="attn_mixer",
    )(h, pre_g, post_g, w_qkv, b_qkv, w_o, b_o, bucket, cap, sinks, rel_bias)


def kernel(x, mix_pre_g, mix_post_g, ffn_pre_g, ffn_post_g, conv_w_in, conv_b_in, conv_dw_w, conv_dw_b, conv_ln_g, conv_ln_b, conv_w_out, conv_b_out, attn_w_qkv, attn_b_qkv, attn_w_o, attn_b_o, attn_sinks, rel_bias, ffn_w_gate_up, ffn_w_down):
    b, s, d = x.shape
    depth = mix_pre_g.shape[0]
    h = x.reshape(b * s, d)
    row = lambda v: v.reshape(1, -1)
    w_gu = ffn_w_gate_up.astype(BF16)
    w_d = ffn_w_down.astype(BF16)
    for i in range(depth):
        j = i // 2
        if i % 2 == 0:
            h = _conv_call(
                h, s, row(mix_pre_g[i]), row(mix_post_g[i]),
                conv_w_in[j].astype(BF16), row(conv_b_in[j]), conv_dw_w[j],
                row(conv_dw_b[j]), row(conv_ln_g[j]), row(conv_ln_b[j]),
                conv_w_out[j].astype(BF16), row(conv_b_out[j]))
        else:
            h = _attn_call(
                h, s, row(mix_pre_g[i]), row(mix_post_g[i]),
                attn_w_qkv[j].astype(BF16), row(attn_b_qkv[j]),
                attn_w_o[j].astype(BF16), row(attn_b_o[j]), attn_sinks[j], rel_bias)
        h = _ffn_call(h, i, row(ffn_pre_g[i]), row(ffn_post_g[i]), w_gu, w_d)
    return h.reshape(b, s, d)
```

```python
import functools
import math

import jax
import jax.numpy as jnp
import numpy as np
from jax import lax
from jax.experimental import pallas as pl
from jax.experimental.pallas import tpu as pltpu

D_MODEL = 1024
CONV_WIDTH = 31
HEAD_DIM = 64
N_Q_HEADS = D_MODEL // HEAD_DIM
N_KV_HEADS = 4
GROUP = N_Q_HEADS // N_KV_HEADS
WINDOW = 128
BLOCK = 128
NUM_BUCKETS = 32
MAX_DISTANCE = 128
D_FF = ((8 * D_MODEL // 3 + 255) // 256) * 256
EPS = 1e-6

V7X_LANES = 128
V7X_SUBLANES = 8

TOKEN_TILE = 512
FFN_TILE = 1024
FFN_ROWS = 256
CONV_HALO = 32
CONV_SUBTILE = 64
VMEM_LIMIT = 56 * 1024 * 1024

F32 = jnp.float32
BF16 = jnp.bfloat16


def _rms(x, g):
    ms = jnp.mean(x * x, axis=-1, keepdims=True)
    return x * lax.rsqrt(ms + EPS) * g


def _sigmoid(x):
    return 1.0 / (1.0 + jnp.exp(-x))


def _const_spec(shape):
    zeros = (0,) * len(shape)
    return pl.BlockSpec(shape, lambda i: zeros, pipeline_mode=pl.Buffered(1))


def _layer_spec(stacked, layer):
    zeros = (0,) * (stacked.ndim - 1)
    return pl.BlockSpec((None,) + stacked.shape[1:], lambda i: (layer,) + zeros,
                        pipeline_mode=pl.Buffered(1))


def _tile_spec(tm, d):
    return pl.BlockSpec((tm, d), lambda i: (i, 0))


def _params():
    return pltpu.CompilerParams(
        dimension_semantics=("arbitrary",), vmem_limit_bytes=VMEM_LIMIT)


def _ffn_kernel(h_ref, pre_ref, post_ref, wgu_ref, wd_ref, o_ref):
    tm = h_ref.shape[0]
    for r0 in range(0, tm, FFN_ROWS):
        rows = slice(r0, r0 + FFN_ROWS)
        h = h_ref[rows, :]
        xn = _rms(h, pre_ref[...]).astype(BF16)
        g = jnp.dot(xn, wgu_ref[:, :D_FF], preferred_element_type=F32)
        u = jnp.dot(xn, wgu_ref[:, D_FF:], preferred_element_type=F32)
        a = (g * _sigmoid(g) * u).astype(BF16)
        f = jnp.dot(a, wd_ref[...], preferred_element_type=F32)
        o_ref[rows, :] = h + _rms(f, post_ref[...])


def _ffn_call(h, layer, pre_g, post_g, w_gu, w_d):
    t, d = h.shape
    tm = FFN_TILE
    assert t % tm == 0 and tm % FFN_ROWS == 0
    return pl.pallas_call(
        _ffn_kernel,
        grid=(t // tm,),
        in_specs=[
            _tile_spec(tm, d),
            _const_spec((1, d)),
            _const_spec((1, d)),
            _layer_spec(w_gu, layer),
            _layer_spec(w_d, layer),
        ],
        out_specs=_tile_spec(tm, d),
        out_shape=jax.ShapeDtypeStruct((t, d), F32),
        compiler_params=_params(),
        name="ffn",
    )(h, pre_g, post_g, w_gu, w_d)


def _conv_kernel(tiles_per_seq, h_ref, pre_ref, post_ref, win_ref, bin_ref,
                 dww_ref, dwb_ref, lng_ref, lnb_ref, wout_ref, bout_ref,
                 o_ref, ext_ref, y_ref):
    tm, d = h_ref.shape
    n_lane = d // V7X_LANES
    i = pl.program_id(0)

    @pl.when(i % tiles_per_seq == 0)
    def _():
        ext_ref[:, 0:CONV_HALO, :] = jnp.zeros((n_lane, CONV_HALO, V7X_LANES), F32)

    h = h_ref[...]
    xn = _rms(h, pre_ref[...]).astype(BF16)
    off0 = CONV_HALO - (CONV_WIDTH - 1)
    ts = CONV_SUBTILE

    def chunk_body(cc, carry):
        for r0 in range(0, tm, ts):
            acc = jnp.broadcast_to(dwb_ref[cc], (ts, V7X_LANES))
            for k in range(CONV_WIDTH):
                acc = acc + ext_ref[cc, r0 + off0 + k:r0 + off0 + k + ts, :] * dww_ref[cc, k:k + 1, :]
            y_ref[cc, r0:r0 + ts, :] = acc
        return carry

    wide = 2 * V7X_LANES
    for c0 in range(0, d, wide):
        val = jnp.dot(xn, win_ref[:, c0:c0 + wide], preferred_element_type=F32) + bin_ref[:, c0:c0 + wide]
        gate = jnp.dot(xn, win_ref[:, d + c0:d + c0 + wide],
                       preferred_element_type=F32) + bin_ref[:, d + c0:d + c0 + wide]
        glu = val * _sigmoid(gate)
        for cc in range(c0 // V7X_LANES, (c0 + wide) // V7X_LANES):
            lo = cc * V7X_LANES - c0
            ext_ref[cc, CONV_HALO:CONV_HALO + tm, :] = glu[:, lo:lo + V7X_LANES]
            chunk_body(cc, 0)

    ext_ref[:, 0:CONV_HALO, :] = ext_ref[:, tm:tm + CONV_HALO, :]

    y = jnp.concatenate([y_ref[cc] for cc in range(n_lane)], axis=1)
    mu = jnp.mean(y, axis=-1, keepdims=True)
    yc = y - mu
    var = jnp.mean(yc * yc, axis=-1, keepdims=True)
    z = yc * lax.rsqrt(var + EPS) * lng_ref[...] + lnb_ref[...]
    z = (z * _sigmoid(z)).astype(BF16)
    f = jnp.dot(z, wout_ref[...], preferred_element_type=F32) + bout_ref[...]
    o_ref[...] = h + _rms(f, post_ref[...])


def _conv_call(h, seq_len, pre_g, post_g, w_in, b_in, dw_w, dw_b, ln_g, ln_b,
               w_out, b_out):
    t, d = h.shape
    tm = TOKEN_TILE
    assert seq_len % tm == 0 and CONV_HALO >= CONV_WIDTH - 1
    n_lane = d // V7X_LANES
    dw_w = dw_w.reshape(CONV_WIDTH, n_lane, V7X_LANES).transpose(1, 0, 2)
    dw_b = dw_b.reshape(n_lane, 1, V7X_LANES)
    return pl.pallas_call(
        functools.partial(_conv_kernel, seq_len // tm),
        grid=(t // tm,),
        in_specs=[
            _tile_spec(tm, d),
            _const_spec((1, d)),
            _const_spec((1, d)),
            _const_spec(w_in.shape),
            _const_spec((1, 2 * d)),
            _const_spec(dw_w.shape),
            _const_spec(dw_b.shape),
            _const_spec((1, d)),
            _const_spec((1, d)),
            _const_spec(w_out.shape),
            _const_spec((1, d)),
        ],
        out_specs=_tile_spec(tm, d),
        out_shape=jax.ShapeDtypeStruct((t, d), F32),
        scratch_shapes=[
            pltpu.VMEM((n_lane, tm + CONV_HALO, V7X_LANES), F32),
            pltpu.VMEM((n_lane, tm, V7X_LANES), F32),
        ],
        compiler_params=_params(),
        name="conv_mixer",
    )(h, pre_g, post_g, w_in, b_in, dw_w, dw_b, ln_g, ln_b, w_out, b_out)


def _attn_kernel(tiles_per_seq, h_ref, pre_ref, post_ref, wqkv_ref, bqkv_ref,
                 wo_ref, bo_ref, bucket_ref, cap_ref, sink_ref, relb_ref,
                 o_ref, q_ref, kw_ref, vw_ref, kc_ref, vc_ref, ao_ref, bias_ref):
    tm, d = h_ref.shape
    nblk = tm // BLOCK
    half = V7X_LANES // 2
    i = pl.program_id(0)
    seq_start = i % tiles_per_seq == 0

    @pl.when(i == 0)
    def _():
        vw_ref[:, :, :, :, V7X_LANES:] = jnp.ones(
            (N_KV_HEADS, 2, nblk, 2 * BLOCK, V7X_LANES), BF16)
        bucket = bucket_ref[...]
        key_col = lax.broadcasted_iota(jnp.int32, bucket.shape, 1)

        def head_body(hh, carry):
            tile = jnp.zeros(bucket.shape, F32)
            for b in range(NUM_BUCKETS):
                tile = jnp.where(bucket == b, relb_ref[b, hh], tile)
            bias_ref[hh] = jnp.where(key_col == 0, sink_ref[hh], tile)
            return carry

        lax.fori_loop(0, N_Q_HEADS, head_body, 0)

    @pl.when(seq_start)
    def _():
        z = jnp.zeros((N_KV_HEADS, 2, BLOCK, V7X_LANES), BF16)
        kc_ref[...] = z
        vc_ref[...] = z

    h = h_ref[...]
    xn = _rms(h, pre_ref[...]).astype(BF16)

    scale = HEAD_DIM ** -0.5
    qkv = jnp.dot(xn, wqkv_ref[...], preferred_element_type=F32) + bqkv_ref[...]
    for kh in range(N_KV_HEADS):
        for pr in range(2):
            c0 = (2 * kh + pr) * V7X_LANES
            qp = (qkv[:, c0:c0 + V7X_LANES] * scale).astype(BF16)
            for j in range(nblk):
                q_ref[kh, j, pr * BLOCK:(pr + 1) * BLOCK, :] = qp[j * BLOCK:(j + 1) * BLOCK, :]

    lane = lax.broadcasted_iota(jnp.int32, (tm, V7X_LANES), 1)
    low = lane < half
    row = lax.broadcasted_iota(jnp.int32, (tm, V7X_LANES), 0)
    block_start = row % BLOCK == 0
    for which, win_ref, carry_ref in ((0, kw_ref, kc_ref), (1, vw_ref, vc_ref)):
        base = N_Q_HEADS * HEAD_DIM + which * N_KV_HEADS * HEAD_DIM
        for col in range(N_KV_HEADS // 2):
            c0 = base + col * V7X_LANES
            x = qkv[:, c0:c0 + V7X_LANES]
            xs = pltpu.roll(x, half, axis=1)
            zero = jnp.zeros_like(x)
            variants = {(2 * col, 0): jnp.where(low, x, zero),
                        (2 * col, 1): jnp.where(low, zero, xs),
                        (2 * col + 1, 0): jnp.where(low, xs, zero),
                        (2 * col + 1, 1): jnp.where(low, zero, x)}
            for (kh, var), val in variants.items():
                cur = val.astype(BF16)
                prv = jnp.where(block_start, zero, val).astype(BF16)
                win_ref[kh, var, 0, 0:BLOCK, 0:V7X_LANES] = carry_ref[kh, var]
                for j in range(nblk):
                    blk = slice(j * BLOCK, (j + 1) * BLOCK)
                    win_ref[kh, var, j, BLOCK:2 * BLOCK, 0:V7X_LANES] = cur[blk]
                    if j + 1 < nblk:
                        win_ref[kh, var, j + 1, 0:BLOCK, 0:V7X_LANES] = prv[blk]
                carry_ref[kh, var] = prv[(nblk - 1) * BLOCK:]

    lane2 = lax.broadcasted_iota(jnp.int32, (2 * BLOCK, V7X_LANES), 1)

    def block_body(j, carry):
        r0 = pl.multiple_of(j * BLOCK, BLOCK)
        first = jnp.logical_and(seq_start, j == 0).astype(jnp.int32)
        for kh in range(N_KV_HEADS):
            q2 = q_ref[kh, j]
            ls = []
            acc = None
            for eo in range(2):
                kb = kw_ref[kh, eo, j]
                vb = vw_ref[kh, eo, j]
                s = lax.dot_general(q2, kb, (((1,), (1,)), ((), ())),
                                    preferred_element_type=F32)
                h_top = 4 * kh + eo
                bias = jnp.concatenate([bias_ref[h_top], bias_ref[h_top + 2]], axis=0)
                s = jnp.minimum(s + bias, cap_ref[first])
                m = jnp.max(s, axis=-1, keepdims=True)
                p = jnp.exp(s - m).astype(BF16)
                pv = jnp.dot(p, vb, preferred_element_type=F32)
                ls.append(pv[:, V7X_LANES:])
                acc = pv[:, :V7X_LANES] if acc is None else acc + pv[:, :V7X_LANES]
            out = acc / jnp.where(lane2 < half, ls[0], ls[1])
            out = out.astype(BF16)
            c0 = kh * 2 * V7X_LANES
            ao_ref[pl.ds(r0, BLOCK), c0:c0 + V7X_LANES] = out[:BLOCK]
            ao_ref[pl.ds(r0, BLOCK), c0 + V7X_LANES:c0 + 2 * V7X_LANES] = out[BLOCK:]
        return carry

    lax.fori_loop(0, nblk, block_body, 0, unroll=True)

    f = jnp.dot(ao_ref[...], wo_ref[...], preferred_element_type=F32) + bo_ref[...]
    o_ref[...] = h + _rms(f, post_ref[...])


def _t5_causal_bucket_np(dist):
    dist = np.maximum(dist, 0)
    max_exact = NUM_BUCKETS // 2
    large = max_exact + (
        np.log(np.maximum(dist, 1).astype(np.float32) / np.float32(max_exact))
        / np.float32(math.log(MAX_DISTANCE / max_exact)) * np.float32(NUM_BUCKETS - max_exact)
    ).astype(np.int32)
    large = np.minimum(large, NUM_BUCKETS - 1)
    return np.where(dist < max_exact, dist, large)


def _attn_tables():
    q_loc = np.arange(BLOCK, dtype=np.int32)[:, None] + BLOCK
    s_loc = np.arange(2 * BLOCK, dtype=np.int32)[None, :]
    dist = q_loc - s_loc
    band = (dist >= 0) & (dist < WINDOW)
    bucket = _t5_causal_bucket_np(dist).astype(np.int32)
    first = band & (s_loc >= BLOCK)
    valid = np.stack([np.tile(band, (2, 1)), np.tile(first, (2, 1))])
    assert not valid[:, :, 0].any()
    valid[:, :, 0] = True
    cap = np.where(valid, np.inf, -np.inf).astype(np.float32)
    return jnp.asarray(bucket), jnp.asarray(cap)


def _attn_call(h, seq_len, pre_g, post_g, w_qkv, b_qkv, w_o, b_o, sinks, rel_bias):
    t, d = h.shape
    tm = TOKEN_TILE
    assert seq_len % tm == 0 and tm % BLOCK == 0
    nblk = tm // BLOCK
    bucket, cap = _attn_tables()
    return pl.pallas_call(
        functools.partial(_attn_kernel, seq_len // tm),
        grid=(t // tm,),
        in_specs=[
            _tile_spec(tm, d),
            _const_spec((1, d)),
            _const_spec((1, d)),
            _const_spec(w_qkv.shape),
            _const_spec((1, w_qkv.shape[1])),
            _const_spec(w_o.shape),
            _const_spec((1, d)),
            _const_spec(bucket.shape),
            _const_spec(cap.shape),
            pl.BlockSpec(memory_space=pltpu.SMEM),
            pl.BlockSpec(memory_space=pltpu.SMEM),
        ],
        out_specs=_tile_spec(tm, d),
        out_shape=jax.ShapeDtypeStruct((t, d), F32),
        scratch_shapes=[
            pltpu.VMEM((N_KV_HEADS, nblk, 2 * BLOCK, V7X_LANES), BF16),
            pltpu.VMEM((N_KV_HEADS, 2, nblk, 2 * BLOCK, V7X_LANES), BF16),
            pltpu.VMEM((N_KV_HEADS, 2, nblk, 2 * BLOCK, 2 * V7X_LANES), BF16),
            pltpu.VMEM((N_KV_HEADS, 2, BLOCK, V7X_LANES), BF16),
            pltpu.VMEM((N_KV_HEADS, 2, BLOCK, V7X_LANES), BF16),
            pltpu.VMEM((tm, d), BF16),
            pltpu.VMEM((N_Q_HEADS, BLOCK, 2 * BLOCK), F32),
        ],
        compiler_params=_params(),
        name="attn_mixer",
    )(h, pre_g, post_g, w_qkv, b_qkv, w_o, b_o, bucket, cap, sinks, rel_bias)


def kernel(x, mix_pre_g, mix_post_g, ffn_pre_g, ffn_post_g, conv_w_in, conv_b_in, conv_dw_w, conv_dw_b, conv_ln_g, conv_ln_b, conv_w_out, conv_b_out, attn_w_qkv, attn_b_qkv, attn_w_o, attn_b_o, attn_sinks, rel_bias, ffn_w_gate_up, ffn_w_down):
    b, s, d = x.shape
    depth = mix_pre_g.shape[0]
    h = x.reshape(b * s, d)
    row = lambda v: v.reshape(1, -1)
    w_gu = ffn_w_gate_up.astype(BF16)
    w_d = ffn_w_down.astype(BF16)
    for i in range(depth):
        j = i // 2
        if i % 2 == 0:
            h = _conv_call(
                h, s, row(mix_pre_g[i]), row(mix_post_g[i]),
                conv_w_in[j].astype(BF16), row(conv_b_in[j]), conv_dw_w[j],
                row(conv_dw_b[j]), row(conv_ln_g[j]), row(conv_ln_b[j]),
                conv_w_out[j].astype(BF16), row(conv_b_out[j]))
        else:
            h = _attn_call(
                h, s, row(mix_pre_g[i]), row(mix_post_g[i]),
                attn_w_qkv[j].astype(BF16), row(attn_b_qkv[j]),
                attn_w_o[j].astype(BF16), row(attn_b_o[j]), attn_sinks[j], rel_bias)
        h = _ffn_call(h, i, row(ffn_pre_g[i]), row(ffn_post_g[i]), w_gu, w_d)
    return h.reshape(b, s, d)
```

```python
import functools
import math

import jax
import jax.numpy as jnp
import numpy as np
from jax import lax
from jax.experimental import pallas as pl
from jax.experimental.pallas import tpu as pltpu

D_MODEL = 1024
CONV_WIDTH = 31
HEAD_DIM = 64
N_Q_HEADS = D_MODEL // HEAD_DIM
N_KV_HEADS = 4
GROUP = N_Q_HEADS // N_KV_HEADS
WINDOW = 128
BLOCK = 128
NUM_BUCKETS = 32
MAX_DISTANCE = 128
D_FF = ((8 * D_MODEL // 3 + 255) // 256) * 256
EPS = 1e-6
LOG2_E = math.log2(math.e)

V7X_LANES = 128
V7X_SUBLANES = 8

TOKEN_TILE = 512
FFN_TILE = 1024
FFN_ROWS = 256
CONV_HALO = 32
CONV_SUBTILE = 64
VMEM_LIMIT = 56 * 1024 * 1024

F32 = jnp.float32
BF16 = jnp.bfloat16


def _rms(x, g):
    ms = jnp.mean(x * x, axis=-1, keepdims=True)
    return x * lax.rsqrt(ms + EPS) * g


def _sigmoid(x):
    return 1.0 / (1.0 + jnp.exp(-x))


def _const_spec(shape):
    zeros = (0,) * len(shape)
    return pl.BlockSpec(shape, lambda i: zeros, pipeline_mode=pl.Buffered(1))


def _layer_spec(stacked, layer):
    zeros = (0,) * (stacked.ndim - 1)
    return pl.BlockSpec((None,) + stacked.shape[1:], lambda i: (layer,) + zeros,
                        pipeline_mode=pl.Buffered(1))


def _tile_spec(tm, d):
    return pl.BlockSpec((tm, d), lambda i: (i, 0))


def _params():
    return pltpu.CompilerParams(
        dimension_semantics=("arbitrary",), vmem_limit_bytes=VMEM_LIMIT)


def _ffn_kernel(h_ref, pre_ref, post_ref, wgu_ref, wd_ref, o_ref):
    tm = h_ref.shape[0]
    for r0 in range(0, tm, FFN_ROWS):
        rows = slice(r0, r0 + FFN_ROWS)
        h = h_ref[rows, :]
        xn = _rms(h, pre_ref[...]).astype(BF16)
        g = jnp.dot(xn, wgu_ref[:, :D_FF], preferred_element_type=F32)
        u = jnp.dot(xn, wgu_ref[:, D_FF:], preferred_element_type=F32)
        a = (g * _sigmoid(g) * u).astype(BF16)
        f = jnp.dot(a, wd_ref[...], preferred_element_type=F32)
        o_ref[rows, :] = h + _rms(f, post_ref[...])


def _ffn_call(h, layer, pre_g, post_g, w_gu, w_d):
    t, d = h.shape
    tm = FFN_TILE
    assert t % tm == 0 and tm % FFN_ROWS == 0
    return pl.pallas_call(
        _ffn_kernel,
        grid=(t // tm,),
        in_specs=[
            _tile_spec(tm, d),
            _const_spec((1, d)),
            _const_spec((1, d)),
            _layer_spec(w_gu, layer),
            _layer_spec(w_d, layer),
        ],
        out_specs=_tile_spec(tm, d),
        out_shape=jax.ShapeDtypeStruct((t, d), F32),
        compiler_params=_params(),
        name="ffn",
    )(h, pre_g, post_g, w_gu, w_d)


def _conv_kernel(tiles_per_seq, h_ref, pre_ref, post_ref, win_ref, bin_ref,
                 dww_ref, dwb_ref, lng_ref, lnb_ref, wout_ref, bout_ref,
                 o_ref, ext_ref, y_ref):
    tm, d = h_ref.shape
    n_lane = d // V7X_LANES
    i = pl.program_id(0)

    @pl.when(i % tiles_per_seq == 0)
    def _():
        ext_ref[:, 0:CONV_HALO, :] = jnp.zeros((n_lane, CONV_HALO, V7X_LANES), F32)

    h = h_ref[...]
    xn = _rms(h, pre_ref[...]).astype(BF16)
    off0 = CONV_HALO - (CONV_WIDTH - 1)
    ts = CONV_SUBTILE

    def chunk_body(cc, carry):
        for r0 in range(0, tm, ts):
            acc = jnp.broadcast_to(dwb_ref[cc], (ts, V7X_LANES))
            for k in range(CONV_WIDTH):
                acc = acc + ext_ref[cc, r0 + off0 + k:r0 + off0 + k + ts, :] * dww_ref[cc, k:k + 1, :]
            y_ref[cc, r0:r0 + ts, :] = acc
        return carry

    wide = 2 * V7X_LANES
    for c0 in range(0, d, wide):
        val = jnp.dot(xn, win_ref[:, c0:c0 + wide], preferred_element_type=F32) + bin_ref[:, c0:c0 + wide]
        gate = jnp.dot(xn, win_ref[:, d + c0:d + c0 + wide],
                       preferred_element_type=F32) + bin_ref[:, d + c0:d + c0 + wide]
        glu = val * _sigmoid(gate)
        for cc in range(c0 // V7X_LANES, (c0 + wide) // V7X_LANES):
            lo = cc * V7X_LANES - c0
            ext_ref[cc, CONV_HALO:CONV_HALO + tm, :] = glu[:, lo:lo + V7X_LANES]
            chunk_body(cc, 0)

    ext_ref[:, 0:CONV_HALO, :] = ext_ref[:, tm:tm + CONV_HALO, :]

    y = jnp.concatenate([y_ref[cc] for cc in range(n_lane)], axis=1)
    mu = jnp.mean(y, axis=-1, keepdims=True)
    yc = y - mu
    var = jnp.mean(yc * yc, axis=-1, keepdims=True)
    z = yc * lax.rsqrt(var + EPS) * lng_ref[...] + lnb_ref[...]
    z = (z * _sigmoid(z)).astype(BF16)
    f = jnp.dot(z, wout_ref[...], preferred_element_type=F32) + bout_ref[...]
    o_ref[...] = h + _rms(f, post_ref[...])


def _conv_call(h, seq_len, pre_g, post_g, w_in, b_in, dw_w, dw_b, ln_g, ln_b,
               w_out, b_out):
    t, d = h.shape
    tm = TOKEN_TILE
    assert seq_len % tm == 0 and CONV_HALO >= CONV_WIDTH - 1
    n_lane = d // V7X_LANES
    dw_w = dw_w.reshape(CONV_WIDTH, n_lane, V7X_LANES).transpose(1, 0, 2)
    dw_b = dw_b.reshape(n_lane, 1, V7X_LANES)
    return pl.pallas_call(
        functools.partial(_conv_kernel, seq_len // tm),
        grid=(t // tm,),
        in_specs=[
            _tile_spec(tm, d),
            _const_spec((1, d)),
            _const_spec((1, d)),
            _const_spec(w_in.shape),
            _const_spec((1, 2 * d)),
            _const_spec(dw_w.shape),
            _const_spec(dw_b.shape),
            _const_spec((1, d)),
            _const_spec((1, d)),
            _const_spec(w_out.shape),
            _const_spec((1, d)),
        ],
        out_specs=_tile_spec(tm, d),
        out_shape=jax.ShapeDtypeStruct((t, d), F32),
        scratch_shapes=[
            pltpu.VMEM((n_lane, tm + CONV_HALO, V7X_LANES), F32),
            pltpu.VMEM((n_lane, tm, V7X_LANES), F32),
        ],
        compiler_params=_params(),
        name="conv_mixer",
    )(h, pre_g, post_g, w_in, b_in, dw_w, dw_b, ln_g, ln_b, w_out, b_out)


def _attn_kernel(tiles_per_seq, h_ref, pre_ref, post_ref, wqkv_ref, bqkv_ref,
                 wo_ref, bo_ref, bucket_ref, cap_ref, sink_ref, relb_ref,
                 o_ref, q_ref, kw_ref, vw_ref, kc_ref, vc_ref, ao_ref, bias_ref):
    tm, d = h_ref.shape
    nblk = tm // BLOCK
    half = V7X_LANES // 2
    i = pl.program_id(0)
    seq_start = i % tiles_per_seq == 0

    @pl.when(i == 0)
    def _():
        vw_ref[:, :, :, :, V7X_LANES:] = jnp.ones(
            (N_KV_HEADS, 2, nblk, 2 * BLOCK, V7X_LANES), BF16)
        bucket = bucket_ref[...]
        key_col = lax.broadcasted_iota(jnp.int32, bucket.shape, 1)

        def head_body(hh, carry):
            tile = jnp.zeros(bucket.shape, F32)
            for b in range(NUM_BUCKETS):
                tile = jnp.where(bucket == b, relb_ref[b, hh], tile)
            bias_ref[hh] = jnp.where(key_col == 0, sink_ref[hh], tile) * LOG2_E
            return carry

        lax.fori_loop(0, N_Q_HEADS, head_body, 0)

    @pl.when(seq_start)
    def _():
        z = jnp.zeros((N_KV_HEADS, 2, BLOCK, V7X_LANES), BF16)
        kc_ref[...] = z
        vc_ref[...] = z

    h = h_ref[...]
    xn = _rms(h, pre_ref[...]).astype(BF16)

    scale = HEAD_DIM ** -0.5 * LOG2_E
    qkv = jnp.dot(xn, wqkv_ref[...], preferred_element_type=F32) + bqkv_ref[...]
    for kh in range(N_KV_HEADS):
        for pr in range(2):
            c0 = (2 * kh + pr) * V7X_LANES
            qp = (qkv[:, c0:c0 + V7X_LANES] * scale).astype(BF16)
            for j in range(nblk):
                q_ref[kh, j, pr * BLOCK:(pr + 1) * BLOCK, :] = qp[j * BLOCK:(j + 1) * BLOCK, :]

    lane = lax.broadcasted_iota(jnp.int32, (tm, V7X_LANES), 1)
    low = lane < half
    row = lax.broadcasted_iota(jnp.int32, (tm, V7X_LANES), 0)
    block_start = row % BLOCK == 0
    for which, win_ref, carry_ref in ((0, kw_ref, kc_ref), (1, vw_ref, vc_ref)):
        base = N_Q_HEADS * HEAD_DIM + which * N_KV_HEADS * HEAD_DIM
        for col in range(N_KV_HEADS // 2):
            c0 = base + col * V7X_LANES
            x = qkv[:, c0:c0 + V7X_LANES]
            xs = pltpu.roll(x, half, axis=1)
            zero = jnp.zeros_like(x)
            variants = {(2 * col, 0): jnp.where(low, x, zero),
                        (2 * col, 1): jnp.where(low, zero, xs),
                        (2 * col + 1, 0): jnp.where(low, xs, zero),
                        (2 * col + 1, 1): jnp.where(low, zero, x)}
            for (kh, var), val in variants.items():
                cur = val.astype(BF16)
                prv = jnp.where(block_start, zero, val).astype(BF16)
                win_ref[kh, var, 0, 0:BLOCK, 0:V7X_LANES] = carry_ref[kh, var]
                for j in range(nblk):
                    blk = slice(j * BLOCK, (j + 1) * BLOCK)
                    win_ref[kh, var, j, BLOCK:2 * BLOCK, 0:V7X_LANES] = cur[blk]
                    if j + 1 < nblk:
                        win_ref[kh, var, j + 1, 0:BLOCK, 0:V7X_LANES] = prv[blk]
                carry_ref[kh, var] = prv[(nblk - 1) * BLOCK:]

    lane2 = lax.broadcasted_iota(jnp.int32, (2 * BLOCK, V7X_LANES), 1)

    def block_body(j, carry):
        r0 = pl.multiple_of(j * BLOCK, BLOCK)
        first = jnp.logical_and(seq_start, j == 0).astype(jnp.int32)
        for kh in range(N_KV_HEADS):
            q2 = q_ref[kh, j]
            ls = []
            acc = None
            for eo in range(2):
                kb = kw_ref[kh, eo, j]
                vb = vw_ref[kh, eo, j]
                s = lax.dot_general(q2, kb, (((1,), (1,)), ((), ())),
                                    preferred_element_type=F32)
                h_top = 4 * kh + eo
                bias = jnp.concatenate([bias_ref[h_top], bias_ref[h_top + 2]], axis=0)
                s = jnp.minimum(s + bias, cap_ref[first])
                m = jnp.max(s, axis=-1, keepdims=True)
                p = jnp.exp2(s - m).astype(BF16)
                pv = jnp.dot(p, vb, preferred_element_type=F32)
                ls.append(pv[:, V7X_LANES:])
                acc = pv[:, :V7X_LANES] if acc is None else acc + pv[:, :V7X_LANES]
            out = acc / jnp.where(lane2 < half, ls[0], ls[1])
            out = out.astype(BF16)
            c0 = kh * 2 * V7X_LANES
            ao_ref[pl.ds(r0, BLOCK), c0:c0 + V7X_LANES] = out[:BLOCK]
            ao_ref[pl.ds(r0, BLOCK), c0 + V7X_LANES:c0 + 2 * V7X_LANES] = out[BLOCK:]
        return carry

    lax.fori_loop(0, nblk, block_body, 0, unroll=True)

    f = jnp.dot(ao_ref[...], wo_ref[...], preferred_element_type=F32) + bo_ref[...]
    o_ref[...] = h + _rms(f, post_ref[...])


def _t5_causal_bucket_np(dist):
    dist = np.maximum(dist, 0)
    max_exact = NUM_BUCKETS // 2
    large = max_exact + (
        np.log(np.maximum(dist, 1).astype(np.float32) / np.float32(max_exact))
        / np.float32(math.log(MAX_DISTANCE / max_exact)) * np.float32(NUM_BUCKETS - max_exact)
    ).astype(np.int32)
    large = np.minimum(large, NUM_BUCKETS - 1)
    return np.where(dist < max_exact, dist, large)


def _attn_tables():
    q_loc = np.arange(BLOCK, dtype=np.int32)[:, None] + BLOCK
    s_loc = np.arange(2 * BLOCK, dtype=np.int32)[None, :]
    dist = q_loc - s_loc
    band = (dist >= 0) & (dist < WINDOW)
    bucket = _t5_causal_bucket_np(dist).astype(np.int32)
    first = band & (s_loc >= BLOCK)
    valid = np.stack([np.tile(band, (2, 1)), np.tile(first, (2, 1))])
    assert not valid[:, :, 0].any()
    valid[:, :, 0] = True
    cap = np.where(valid, np.inf, -np.inf).astype(np.float32)
    return jnp.asarray(bucket), jnp.asarray(cap)


def _attn_call(h, seq_len, pre_g, post_g, w_qkv, b_qkv, w_o, b_o, sinks, rel_bias):
    t, d = h.shape
    tm = TOKEN_TILE
    assert seq_len % tm == 0 and tm % BLOCK == 0
    nblk = tm // BLOCK
    bucket, cap = _attn_tables()
    return pl.pallas_call(
        functools.partial(_attn_kernel, seq_len // tm),
        grid=(t // tm,),
        in_specs=[
            _tile_spec(tm, d),
            _const_spec((1, d)),
            _const_spec((1, d)),
            _const_spec(w_qkv.shape),
            _const_spec((1, w_qkv.shape[1])),
            _const_spec(w_o.shape),
            _const_spec((1, d)),
            _const_spec(bucket.shape),
            _const_spec(cap.shape),
            pl.BlockSpec(memory_space=pltpu.SMEM),
            pl.BlockSpec(memory_space=pltpu.SMEM),
        ],
        out_specs=_tile_spec(tm, d),
        out_shape=jax.ShapeDtypeStruct((t, d), F32),
        scratch_shapes=[
            pltpu.VMEM((N_KV_HEADS, nblk, 2 * BLOCK, V7X_LANES), BF16),
            pltpu.VMEM((N_KV_HEADS, 2, nblk, 2 * BLOCK, V7X_LANES), BF16),
            pltpu.VMEM((N_KV_HEADS, 2, nblk, 2 * BLOCK, 2 * V7X_LANES), BF16),
            pltpu.VMEM((N_KV_HEADS, 2, BLOCK, V7X_LANES), BF16),
            pltpu.VMEM((N_KV_HEADS, 2, BLOCK, V7X_LANES), BF16),
            pltpu.VMEM((tm, d), BF16),
            pltpu.VMEM((N_Q_HEADS, BLOCK, 2 * BLOCK), F32),
        ],
        compiler_params=_params(),
        name="attn_mixer",
    )(h, pre_g, post_g, w_qkv, b_qkv, w_o, b_o, bucket, cap, sinks, rel_bias)


def kernel(x, mix_pre_g, mix_post_g, ffn_pre_g, ffn_post_g, conv_w_in, conv_b_in, conv_dw_w, conv_dw_b, conv_ln_g, conv_ln_b, conv_w_out, conv_b_out, attn_w_qkv, attn_b_qkv, attn_w_o, attn_b_o, attn_sinks, rel_bias, ffn_w_gate_up, ffn_w_down):
    b, s, d = x.shape
    depth = mix_pre_g.shape[0]
    h = x.reshape(b * s, d)
    row = lambda v: v.reshape(1, -1)
    w_gu = ffn_w_gate_up.astype(BF16)
    w_d = ffn_w_down.astype(BF16)
    for i in range(depth):
        j = i // 2
        if i % 2 == 0:
            h = _conv_call(
                h, s, row(mix_pre_g[i]), row(mix_post_g[i]),
                conv_w_in[j].astype(BF16), row(conv_b_in[j]), conv_dw_w[j],
                row(conv_dw_b[j]), row(conv_ln_g[j]), row(conv_ln_b[j]),
                conv_w_out[j].astype(BF16), row(conv_b_out[j]))
        else:
            h = _attn_call(
                h, s, row(mix_pre_g[i]), row(mix_post_g[i]),
                attn_w_qkv[j].astype(BF16), row(attn_b_qkv[j]),
                attn_w_o[j].astype(BF16), row(attn_b_o[j]), attn_sinks[j], rel_bias)
        h = _ffn_call(h, i, row(ffn_pre_g[i]), row(ffn_post_g[i]), w_gu, w_d)
    return h.reshape(b, s, d)
```

```python
import functools
import math

import jax
import jax.numpy as jnp
import numpy as np
from jax import lax
from jax.experimental import pallas as pl
from jax.experimental.pallas import tpu as pltpu

D_MODEL = 1024
CONV_WIDTH = 31
HEAD_DIM = 64
N_Q_HEADS = D_MODEL // HEAD_DIM
N_KV_HEADS = 4
GROUP = N_Q_HEADS // N_KV_HEADS
WINDOW = 128
BLOCK = 128
NUM_BUCKETS = 32
MAX_DISTANCE = 128
D_FF = ((8 * D_MODEL // 3 + 255) // 256) * 256
EPS = 1e-6
LOG2_E = math.log2(math.e)

V7X_LANES = 128
V7X_SUBLANES = 8

TOKEN_TILE = 1024
FFN_TILE = 1024
FFN_ROWS = 256
CONV_HALO = 32
CONV_SUBTILE = 64
VMEM_LIMIT = 56 * 1024 * 1024

F32 = jnp.float32
BF16 = jnp.bfloat16


def _rms(x, g):
    ms = jnp.mean(x * x, axis=-1, keepdims=True)
    return x * lax.rsqrt(ms + EPS) * g


def _sigmoid(x):
    return 1.0 / (1.0 + jnp.exp(-x))


def _const_spec(shape):
    zeros = (0,) * len(shape)
    return pl.BlockSpec(shape, lambda i: zeros, pipeline_mode=pl.Buffered(1))


def _layer_spec(stacked, layer):
    zeros = (0,) * (stacked.ndim - 1)
    return pl.BlockSpec((None,) + stacked.shape[1:], lambda i: (layer,) + zeros,
                        pipeline_mode=pl.Buffered(1))


def _tile_spec(tm, d):
    return pl.BlockSpec((tm, d), lambda i: (i, 0))


def _params():
    return pltpu.CompilerParams(
        dimension_semantics=("arbitrary",), vmem_limit_bytes=VMEM_LIMIT)


def _ffn_kernel(h_ref, pre_ref, post_ref, wgu_ref, wd_ref, o_ref):
    tm = h_ref.shape[0]
    for r0 in range(0, tm, FFN_ROWS):
        rows = slice(r0, r0 + FFN_ROWS)
        h = h_ref[rows, :]
        xn = _rms(h, pre_ref[...]).astype(BF16)
        g = jnp.dot(xn, wgu_ref[:, :D_FF], preferred_element_type=F32)
        u = jnp.dot(xn, wgu_ref[:, D_FF:], preferred_element_type=F32)
        a = (g * _sigmoid(g) * u).astype(BF16)
        f = jnp.dot(a, wd_ref[...], preferred_element_type=F32)
        o_ref[rows, :] = h + _rms(f, post_ref[...])


def _ffn_call(h, layer, pre_g, post_g, w_gu, w_d):
    t, d = h.shape
    tm = FFN_TILE
    assert t % tm == 0 and tm % FFN_ROWS == 0
    return pl.pallas_call(
        _ffn_kernel,
        grid=(t // tm,),
        in_specs=[
            _tile_spec(tm, d),
            _const_spec((1, d)),
            _const_spec((1, d)),
            _layer_spec(w_gu, layer),
            _layer_spec(w_d, layer),
        ],
        out_specs=_tile_spec(tm, d),
        out_shape=jax.ShapeDtypeStruct((t, d), F32),
        compiler_params=_params(),
        name="ffn",
    )(h, pre_g, post_g, w_gu, w_d)


def _conv_kernel(tiles_per_seq, h_ref, pre_ref, post_ref, win_ref, bin_ref,
                 dww_ref, dwb_ref, lng_ref, lnb_ref, wout_ref, bout_ref,
                 o_ref, ext_ref, y_ref):
    tm, d = h_ref.shape
    n_lane = d // V7X_LANES
    i = pl.program_id(0)

    @pl.when(i % tiles_per_seq == 0)
    def _():
        ext_ref[:, 0:CONV_HALO, :] = jnp.zeros((n_lane, CONV_HALO, V7X_LANES), F32)

    h = h_ref[...]
    xn = _rms(h, pre_ref[...]).astype(BF16)
    off0 = CONV_HALO - (CONV_WIDTH - 1)
    ts = CONV_SUBTILE

    def chunk_body(cc, carry):
        for r0 in range(0, tm, ts):
            acc = jnp.broadcast_to(dwb_ref[cc], (ts, V7X_LANES))
            for k in range(CONV_WIDTH):
                acc = acc + ext_ref[cc, r0 + off0 + k:r0 + off0 + k + ts, :] * dww_ref[cc, k:k + 1, :]
            y_ref[cc, r0:r0 + ts, :] = acc
        return carry

    wide = 2 * V7X_LANES
    for c0 in range(0, d, wide):
        val = jnp.dot(xn, win_ref[:, c0:c0 + wide], preferred_element_type=F32) + bin_ref[:, c0:c0 + wide]
        gate = jnp.dot(xn, win_ref[:, d + c0:d + c0 + wide],
                       preferred_element_type=F32) + bin_ref[:, d + c0:d + c0 + wide]
        glu = val * _sigmoid(gate)
        for cc in range(c0 // V7X_LANES, (c0 + wide) // V7X_LANES):
            lo = cc * V7X_LANES - c0
            ext_ref[cc, CONV_HALO:CONV_HALO + tm, :] = glu[:, lo:lo + V7X_LANES]
            chunk_body(cc, 0)

    ext_ref[:, 0:CONV_HALO, :] = ext_ref[:, tm:tm + CONV_HALO, :]

    y = jnp.concatenate([y_ref[cc] for cc in range(n_lane)], axis=1)
    mu = jnp.mean(y, axis=-1, keepdims=True)
    yc = y - mu
    var = jnp.mean(yc * yc, axis=-1, keepdims=True)
    z = yc * lax.rsqrt(var + EPS) * lng_ref[...] + lnb_ref[...]
    z = (z * _sigmoid(z)).astype(BF16)
    f = jnp.dot(z, wout_ref[...], preferred_element_type=F32) + bout_ref[...]
    o_ref[...] = h + _rms(f, post_ref[...])


def _conv_call(h, seq_len, pre_g, post_g, w_in, b_in, dw_w, dw_b, ln_g, ln_b,
               w_out, b_out):
    t, d = h.shape
    tm = TOKEN_TILE
    assert seq_len % tm == 0 and CONV_HALO >= CONV_WIDTH - 1
    n_lane = d // V7X_LANES
    dw_w = dw_w.reshape(CONV_WIDTH, n_lane, V7X_LANES).transpose(1, 0, 2)
    dw_b = dw_b.reshape(n_lane, 1, V7X_LANES)
    return pl.pallas_call(
        functools.partial(_conv_kernel, seq_len // tm),
        grid=(t // tm,),
        in_specs=[
            _tile_spec(tm, d),
            _const_spec((1, d)),
            _const_spec((1, d)),
            _const_spec(w_in.shape),
            _const_spec((1, 2 * d)),
            _const_spec(dw_w.shape),
            _const_spec(dw_b.shape),
            _const_spec((1, d)),
            _const_spec((1, d)),
            _const_spec(w_out.shape),
            _const_spec((1, d)),
        ],
        out_specs=_tile_spec(tm, d),
        out_shape=jax.ShapeDtypeStruct((t, d), F32),
        scratch_shapes=[
            pltpu.VMEM((n_lane, tm + CONV_HALO, V7X_LANES), F32),
            pltpu.VMEM((n_lane, tm, V7X_LANES), F32),
        ],
        compiler_params=_params(),
        name="conv_mixer",
    )(h, pre_g, post_g, w_in, b_in, dw_w, dw_b, ln_g, ln_b, w_out, b_out)


def _attn_kernel(tiles_per_seq, h_ref, pre_ref, post_ref, wqkv_ref, bqkv_ref,
                 wo_ref, bo_ref, bucket_ref, cap_ref, sink_ref, relb_ref,
                 o_ref, q_ref, kw_ref, vw_ref, kc_ref, vc_ref, ao_ref, bias_ref):
    tm, d = h_ref.shape
    nblk = tm // BLOCK
    half = V7X_LANES // 2
    i = pl.program_id(0)
    seq_start = i % tiles_per_seq == 0

    @pl.when(i == 0)
    def _():
        vw_ref[:, :, :, :, V7X_LANES:] = jnp.ones(
            (N_KV_HEADS, 2, nblk, 2 * BLOCK, V7X_LANES), BF16)
        bucket = bucket_ref[...]
        key_col = lax.broadcasted_iota(jnp.int32, bucket.shape, 1)

        def head_body(hh, carry):
            tile = jnp.zeros(bucket.shape, F32)
            for b in range(NUM_BUCKETS):
                tile = jnp.where(bucket == b, relb_ref[b, hh], tile)
            bias_ref[hh] = jnp.where(key_col == 0, sink_ref[hh], tile) * LOG2_E
            return carry

        lax.fori_loop(0, N_Q_HEADS, head_body, 0)

    @pl.when(seq_start)
    def _():
        z = jnp.zeros((N_KV_HEADS, 2, BLOCK, V7X_LANES), BF16)
        kc_ref[...] = z
        vc_ref[...] = z

    h = h_ref[...]
    xn = _rms(h, pre_ref[...]).astype(BF16)

    scale = HEAD_DIM ** -0.5 * LOG2_E
    qkv = jnp.dot(xn, wqkv_ref[...], preferred_element_type=F32) + bqkv_ref[...]
    for kh in range(N_KV_HEADS):
        for pr in range(2):
            c0 = (2 * kh + pr) * V7X_LANES
            qp = (qkv[:, c0:c0 + V7X_LANES] * scale).astype(BF16)
            for j in range(nblk):
                q_ref[kh, j, pr * BLOCK:(pr + 1) * BLOCK, :] = qp[j * BLOCK:(j + 1) * BLOCK, :]

    lane = lax.broadcasted_iota(jnp.int32, (tm, V7X_LANES), 1)
    low = lane < half
    row = lax.broadcasted_iota(jnp.int32, (tm, V7X_LANES), 0)
    block_start = row % BLOCK == 0
    for which, win_ref, carry_ref in ((0, kw_ref, kc_ref), (1, vw_ref, vc_ref)):
        base = N_Q_HEADS * HEAD_DIM + which * N_KV_HEADS * HEAD_DIM
        for col in range(N_KV_HEADS // 2):
            c0 = base + col * V7X_LANES
            x = qkv[:, c0:c0 + V7X_LANES]
            xs = pltpu.roll(x, half, axis=1)
            zero = jnp.zeros_like(x)
            variants = {(2 * col, 0): jnp.where(low, x, zero),
                        (2 * col, 1): jnp.where(low, zero, xs),
                        (2 * col + 1, 0): jnp.where(low, xs, zero),
                        (2 * col + 1, 1): jnp.where(low, zero, x)}
            for (kh, var), val in variants.items():
                cur = val.astype(BF16)
                prv = jnp.where(block_start, zero, val).astype(BF16)
                win_ref[kh, var, 0, 0:BLOCK, 0:V7X_LANES] = carry_ref[kh, var]
                for j in range(nblk):
                    blk = slice(j * BLOCK, (j + 1) * BLOCK)
                    win_ref[kh, var, j, BLOCK:2 * BLOCK, 0:V7X_LANES] = cur[blk]
                    if j + 1 < nblk:
                        win_ref[kh, var, j + 1, 0:BLOCK, 0:V7X_LANES] = prv[blk]
                carry_ref[kh, var] = prv[(nblk - 1) * BLOCK:]

    lane2 = lax.broadcasted_iota(jnp.int32, (2 * BLOCK, V7X_LANES), 1)

    def block_body(j, carry):
        r0 = pl.multiple_of(j * BLOCK, BLOCK)
        first = jnp.logical_and(seq_start, j == 0).astype(jnp.int32)
        for kh in range(N_KV_HEADS):
            q2 = q_ref[kh, j]
            ls = []
            acc = None
            for eo in range(2):
                kb = kw_ref[kh, eo, j]
                vb = vw_ref[kh, eo, j]
                s = lax.dot_general(q2, kb, (((1,), (1,)), ((), ())),
                                    preferred_element_type=F32)
                h_top = 4 * kh + eo
                bias = jnp.concatenate([bias_ref[h_top], bias_ref[h_top + 2]], axis=0)
                s = jnp.minimum(s + bias, cap_ref[first])
                m = jnp.max(s, axis=-1, keepdims=True)
                p = jnp.exp2(s - m).astype(BF16)
                pv = jnp.dot(p, vb, preferred_element_type=F32)
                ls.append(pv[:, V7X_LANES:])
                acc = pv[:, :V7X_LANES] if acc is None else acc + pv[:, :V7X_LANES]
            out = acc / jnp.where(lane2 < half, ls[0], ls[1])
            out = out.astype(BF16)
            c0 = kh * 2 * V7X_LANES
            ao_ref[pl.ds(r0, BLOCK), c0:c0 + V7X_LANES] = out[:BLOCK]
            ao_ref[pl.ds(r0, BLOCK), c0 + V7X_LANES:c0 + 2 * V7X_LANES] = out[BLOCK:]
        return carry

    lax.fori_loop(0, nblk, block_body, 0, unroll=True)

    f = jnp.dot(ao_ref[...], wo_ref[...], preferred_element_type=F32) + bo_ref[...]
    o_ref[...] = h + _rms(f, post_ref[...])


def _t5_causal_bucket_np(dist):
    dist = np.maximum(dist, 0)
    max_exact = NUM_BUCKETS // 2
    large = max_exact + (
        np.log(np.maximum(dist, 1).astype(np.float32) / np.float32(max_exact))
        / np.float32(math.log(MAX_DISTANCE / max_exact)) * np.float32(NUM_BUCKETS - max_exact)
    ).astype(np.int32)
    large = np.minimum(large, NUM_BUCKETS - 1)
    return np.where(dist < max_exact, dist, large)


def _attn_tables():
    q_loc = np.arange(BLOCK, dtype=np.int32)[:, None] + BLOCK
    s_loc = np.arange(2 * BLOCK, dtype=np.int32)[None, :]
    dist = q_loc - s_loc
    band = (dist >= 0) & (dist < WINDOW)
    bucket = _t5_causal_bucket_np(dist).astype(np.int32)
    first = band & (s_loc >= BLOCK)
    valid = np.stack([np.tile(band, (2, 1)), np.tile(first, (2, 1))])
    assert not valid[:, :, 0].any()
    valid[:, :, 0] = True
    cap = np.where(valid, np.inf, -np.inf).astype(np.float32)
    return jnp.asarray(bucket), jnp.asarray(cap)


def _attn_call(h, seq_len, pre_g, post_g, w_qkv, b_qkv, w_o, b_o, sinks, rel_bias):
    t, d = h.shape
    tm = TOKEN_TILE
    assert seq_len % tm == 0 and tm % BLOCK == 0
    nblk = tm // BLOCK
    bucket, cap = _attn_tables()
    return pl.pallas_call(
        functools.partial(_attn_kernel, seq_len // tm),
        grid=(t // tm,),
        in_specs=[
            _tile_spec(tm, d),
            _const_spec((1, d)),
            _const_spec((1, d)),
            _const_spec(w_qkv.shape),
            _const_spec((1, w_qkv.shape[1])),
            _const_spec(w_o.shape),
            _const_spec((1, d)),
            _const_spec(bucket.shape),
            _const_spec(cap.shape),
            pl.BlockSpec(memory_space=pltpu.SMEM),
            pl.BlockSpec(memory_space=pltpu.SMEM),
        ],
        out_specs=_tile_spec(tm, d),
        out_shape=jax.ShapeDtypeStruct((t, d), F32),
        scratch_shapes=[
            pltpu.VMEM((N_KV_HEADS, nblk, 2 * BLOCK, V7X_LANES), BF16),
            pltpu.VMEM((N_KV_HEADS, 2, nblk, 2 * BLOCK, V7X_LANES), BF16),
            pltpu.VMEM((N_KV_HEADS, 2, nblk, 2 * BLOCK, 2 * V7X_LANES), BF16),
            pltpu.VMEM((N_KV_HEADS, 2, BLOCK, V7X_LANES), BF16),
            pltpu.VMEM((N_KV_HEADS, 2, BLOCK, V7X_LANES), BF16),
            pltpu.VMEM((tm, d), BF16),
            pltpu.VMEM((N_Q_HEADS, BLOCK, 2 * BLOCK), F32),
        ],
        compiler_params=_params(),
        name="attn_mixer",
    )(h, pre_g, post_g, w_qkv, b_qkv, w_o, b_o, bucket, cap, sinks, rel_bias)


def kernel(x, mix_pre_g, mix_post_g, ffn_pre_g, ffn_post_g, conv_w_in, conv_b_in, conv_dw_w, conv_dw_b, conv_ln_g, conv_ln_b, conv_w_out, conv_b_out, attn_w_qkv, attn_b_qkv, attn_w_o, attn_b_o, attn_sinks, rel_bias, ffn_w_gate_up, ffn_w_down):
    b, s, d = x.shape
    depth = mix_pre_g.shape[0]
    h = x.reshape(b * s, d)
    row = lambda v: v.reshape(1, -1)
    w_gu = ffn_w_gate_up.astype(BF16)
    w_d = ffn_w_down.astype(BF16)
    for i in range(depth):
        j = i // 2
        if i % 2 == 0:
            h = _conv_call(
                h, s, row(mix_pre_g[i]), row(mix_post_g[i]),
                conv_w_in[j].astype(BF16), row(conv_b_in[j]), conv_dw_w[j],
                row(conv_dw_b[j]), row(conv_ln_g[j]), row(conv_ln_b[j]),
                conv_w_out[j].astype(BF16), row(conv_b_out[j]))
        else:
            h = _attn_call(
                h, s, row(mix_pre_g[i]), row(mix_post_g[i]),
                attn_w_qkv[j].astype(BF16), row(attn_b_qkv[j]),
                attn_w_o[j].astype(BF16), row(attn_b_o[j]), attn_sinks[j], rel_bias)
        h = _ffn_call(h, i, row(ffn_pre_g[i]), row(ffn_post_g[i]), w_gu, w_d)
    return h.reshape(b, s, d)
```

```python
import functools
import math

import jax
import jax.numpy as jnp
import numpy as np
from jax import lax
from jax.experimental import pallas as pl
from jax.experimental.pallas import tpu as pltpu

D_MODEL = 1024
CONV_WIDTH = 31
HEAD_DIM = 64
N_Q_HEADS = D_MODEL // HEAD_DIM
N_KV_HEADS = 4
GROUP = N_Q_HEADS // N_KV_HEADS
WINDOW = 128
BLOCK = 128
NUM_BUCKETS = 32
MAX_DISTANCE = 128
D_FF = ((8 * D_MODEL // 3 + 255) // 256) * 256
EPS = 1e-6
LOG2_E = math.log2(math.e)

V7X_LANES = 128
V7X_SUBLANES = 8

TOKEN_TILE = 1024
FFN_TILE = 1024
FFN_ROWS = 256
CONV_HALO = 32
CONV_SUBTILE = 64
VMEM_LIMIT = 56 * 1024 * 1024

F32 = jnp.float32
BF16 = jnp.bfloat16


def _rms(x, g):
    ms = jnp.mean(x * x, axis=-1, keepdims=True)
    return x * lax.rsqrt(ms + EPS) * g


def _sigmoid(x):
    return 1.0 / (1.0 + jnp.exp2(x * -LOG2_E))


def _const_spec(shape):
    zeros = (0,) * len(shape)
    return pl.BlockSpec(shape, lambda i: zeros, pipeline_mode=pl.Buffered(1))


def _layer_spec(stacked, layer):
    zeros = (0,) * (stacked.ndim - 1)
    return pl.BlockSpec((None,) + stacked.shape[1:], lambda i: (layer,) + zeros,
                        pipeline_mode=pl.Buffered(1))


def _tile_spec(tm, d):
    return pl.BlockSpec((tm, d), lambda i: (i, 0))


def _params():
    return pltpu.CompilerParams(
        dimension_semantics=("arbitrary",), vmem_limit_bytes=VMEM_LIMIT)


def _ffn_kernel(h_ref, pre_ref, post_ref, wgu_ref, wd_ref, o_ref):
    tm = h_ref.shape[0]
    for r0 in range(0, tm, FFN_ROWS):
        rows = slice(r0, r0 + FFN_ROWS)
        h = h_ref[rows, :]
        xn = _rms(h, pre_ref[...]).astype(BF16)
        g = jnp.dot(xn, wgu_ref[:, :D_FF], preferred_element_type=F32)
        u = jnp.dot(xn, wgu_ref[:, D_FF:], preferred_element_type=F32)
        a = (g * _sigmoid(g) * u).astype(BF16)
        f = jnp.dot(a, wd_ref[...], preferred_element_type=F32)
        o_ref[rows, :] = h + _rms(f, post_ref[...])


def _ffn_call(h, layer, pre_g, post_g, w_gu, w_d):
    t, d = h.shape
    tm = FFN_TILE
    assert t % tm == 0 and tm % FFN_ROWS == 0
    return pl.pallas_call(
        _ffn_kernel,
        grid=(t // tm,),
        in_specs=[
            _tile_spec(tm, d),
            _const_spec((1, d)),
            _const_spec((1, d)),
            _layer_spec(w_gu, layer),
            _layer_spec(w_d, layer),
        ],
        out_specs=_tile_spec(tm, d),
        out_shape=jax.ShapeDtypeStruct((t, d), F32),
        compiler_params=_params(),
        name="ffn",
    )(h, pre_g, post_g, w_gu, w_d)


def _conv_kernel(tiles_per_seq, h_ref, pre_ref, post_ref, win_ref, bin_ref,
                 dww_ref, dwb_ref, lng_ref, lnb_ref, wout_ref, bout_ref,
                 o_ref, ext_ref, y_ref):
    tm, d = h_ref.shape
    n_lane = d // V7X_LANES
    i = pl.program_id(0)

    @pl.when(i % tiles_per_seq == 0)
    def _():
        ext_ref[:, 0:CONV_HALO, :] = jnp.zeros((n_lane, CONV_HALO, V7X_LANES), F32)

    h = h_ref[...]
    xn = _rms(h, pre_ref[...]).astype(BF16)
    off0 = CONV_HALO - (CONV_WIDTH - 1)
    ts = CONV_SUBTILE

    def chunk_body(cc, carry):
        for r0 in range(0, tm, ts):
            acc = jnp.broadcast_to(dwb_ref[cc], (ts, V7X_LANES))
            for k in range(CONV_WIDTH):
                acc = acc + ext_ref[cc, r0 + off0 + k:r0 + off0 + k + ts, :] * dww_ref[cc, k:k + 1, :]
            y_ref[cc, r0:r0 + ts, :] = acc
        return carry

    wide = 2 * V7X_LANES
    for c0 in range(0, d, wide):
        val = jnp.dot(xn, win_ref[:, c0:c0 + wide], preferred_element_type=F32) + bin_ref[:, c0:c0 + wide]
        gate = jnp.dot(xn, win_ref[:, d + c0:d + c0 + wide],
                       preferred_element_type=F32) + bin_ref[:, d + c0:d + c0 + wide]
        glu = val * _sigmoid(gate)
        for cc in range(c0 // V7X_LANES, (c0 + wide) // V7X_LANES):
            lo = cc * V7X_LANES - c0
            ext_ref[cc, CONV_HALO:CONV_HALO + tm, :] = glu[:, lo:lo + V7X_LANES]
            chunk_body(cc, 0)

    ext_ref[:, 0:CONV_HALO, :] = ext_ref[:, tm:tm + CONV_HALO, :]

    y = jnp.concatenate([y_ref[cc] for cc in range(n_lane)], axis=1)
    mu = jnp.mean(y, axis=-1, keepdims=True)
    yc = y - mu
    var = jnp.mean(yc * yc, axis=-1, keepdims=True)
    z = yc * lax.rsqrt(var + EPS) * lng_ref[...] + lnb_ref[...]
    z = (z * _sigmoid(z)).astype(BF16)
    f = jnp.dot(z, wout_ref[...], preferred_element_type=F32) + bout_ref[...]
    o_ref[...] = h + _rms(f, post_ref[...])


def _conv_call(h, seq_len, pre_g, post_g, w_in, b_in, dw_w, dw_b, ln_g, ln_b,
               w_out, b_out):
    t, d = h.shape
    tm = TOKEN_TILE
    assert seq_len % tm == 0 and CONV_HALO >= CONV_WIDTH - 1
    n_lane = d // V7X_LANES
    dw_w = dw_w.reshape(CONV_WIDTH, n_lane, V7X_LANES).transpose(1, 0, 2)
    dw_b = dw_b.reshape(n_lane, 1, V7X_LANES)
    return pl.pallas_call(
        functools.partial(_conv_kernel, seq_len // tm),
        grid=(t // tm,),
        in_specs=[
            _tile_spec(tm, d),
            _const_spec((1, d)),
            _const_spec((1, d)),
            _const_spec(w_in.shape),
            _const_spec((1, 2 * d)),
            _const_spec(dw_w.shape),
            _const_spec(dw_b.shape),
            _const_spec((1, d)),
            _const_spec((1, d)),
            _const_spec(w_out.shape),
            _const_spec((1, d)),
        ],
        out_specs=_tile_spec(tm, d),
        out_shape=jax.ShapeDtypeStruct((t, d), F32),
        scratch_shapes=[
            pltpu.VMEM((n_lane, tm + CONV_HALO, V7X_LANES), F32),
            pltpu.VMEM((n_lane, tm, V7X_LANES), F32),
        ],
        compiler_params=_params(),
        name="conv_mixer",
    )(h, pre_g, post_g, w_in, b_in, dw_w, dw_b, ln_g, ln_b, w_out, b_out)


def _attn_kernel(tiles_per_seq, h_ref, pre_ref, post_ref, wqkv_ref, bqkv_ref,
                 wo_ref, bo_ref, bucket_ref, cap_ref, sink_ref, relb_ref,
                 o_ref, q_ref, kw_ref, vw_ref, kc_ref, vc_ref, ao_ref, bias_ref):
    tm, d = h_ref.shape
    nblk = tm // BLOCK
    half = V7X_LANES // 2
    i = pl.program_id(0)
    seq_start = i % tiles_per_seq == 0

    @pl.when(i == 0)
    def _():
        vw_ref[:, :, :, :, V7X_LANES:] = jnp.ones(
            (N_KV_HEADS, 2, nblk, 2 * BLOCK, V7X_LANES), BF16)
        bucket = bucket_ref[...]
        key_col = lax.broadcasted_iota(jnp.int32, bucket.shape, 1)

        def head_body(hh, carry):
            tile = jnp.zeros(bucket.shape, F32)
            for b in range(NUM_BUCKETS):
                tile = jnp.where(bucket == b, relb_ref[b, hh], tile)
            bias_ref[hh] = jnp.where(key_col == 0, sink_ref[hh], tile) * LOG2_E
            return carry

        lax.fori_loop(0, N_Q_HEADS, head_body, 0)

    @pl.when(seq_start)
    def _():
        z = jnp.zeros((N_KV_HEADS, 2, BLOCK, V7X_LANES), BF16)
        kc_ref[...] = z
        vc_ref[...] = z

    h = h_ref[...]
    xn = _rms(h, pre_ref[...]).astype(BF16)

    scale = HEAD_DIM ** -0.5 * LOG2_E
    qkv = jnp.dot(xn, wqkv_ref[...], preferred_element_type=F32) + bqkv_ref[...]
    for kh in range(N_KV_HEADS):
        for pr in range(2):
            c0 = (2 * kh + pr) * V7X_LANES
            qp = (qkv[:, c0:c0 + V7X_LANES] * scale).astype(BF16)
            for j in range(nblk):
                q_ref[kh, j, pr * BLOCK:(pr + 1) * BLOCK, :] = qp[j * BLOCK:(j + 1) * BLOCK, :]

    lane = lax.broadcasted_iota(jnp.int32, (tm, V7X_LANES), 1)
    low = lane < half
    row = lax.broadcasted_iota(jnp.int32, (tm, V7X_LANES), 0)
    block_start = row % BLOCK == 0
    for which, win_ref, carry_ref in ((0, kw_ref, kc_ref), (1, vw_ref, vc_ref)):
        base = N_Q_HEADS * HEAD_DIM + which * N_KV_HEADS * HEAD_DIM
        for col in range(N_KV_HEADS // 2):
            c0 = base + col * V7X_LANES
            x = qkv[:, c0:c0 + V7X_LANES]
            xs = pltpu.roll(x, half, axis=1)
            zero = jnp.zeros_like(x)
            variants = {(2 * col, 0): jnp.where(low, x, zero),
                        (2 * col, 1): jnp.where(low, zero, xs),
                        (2 * col + 1, 0): jnp.where(low, xs, zero),
                        (2 * col + 1, 1): jnp.where(low, zero, x)}
            for (kh, var), val in variants.items():
                cur = val.astype(BF16)
                prv = jnp.where(block_start, zero, val).astype(BF16)
                win_ref[kh, var, 0, 0:BLOCK, 0:V7X_LANES] = carry_ref[kh, var]
                for j in range(nblk):
                    blk = slice(j * BLOCK, (j + 1) * BLOCK)
                    win_ref[kh, var, j, BLOCK:2 * BLOCK, 0:V7X_LANES] = cur[blk]
                    if j + 1 < nblk:
                        win_ref[kh, var, j + 1, 0:BLOCK, 0:V7X_LANES] = prv[blk]
                carry_ref[kh, var] = prv[(nblk - 1) * BLOCK:]

    lane2 = lax.broadcasted_iota(jnp.int32, (2 * BLOCK, V7X_LANES), 1)

    def block_body(j, carry):
        r0 = pl.multiple_of(j * BLOCK, BLOCK)
        first = jnp.logical_and(seq_start, j == 0).astype(jnp.int32)
        for kh in range(N_KV_HEADS):
            q2 = q_ref[kh, j]
            ls = []
            acc = None
            for eo in range(2):
                kb = kw_ref[kh, eo, j]
                vb = vw_ref[kh, eo, j]
                s = lax.dot_general(q2, kb, (((1,), (1,)), ((), ())),
                                    preferred_element_type=F32)
                h_top = 4 * kh + eo
                bias = jnp.concatenate([bias_ref[h_top], bias_ref[h_top + 2]], axis=0)
                s = jnp.minimum(s + bias, cap_ref[first])
                m = jnp.max(s, axis=-1, keepdims=True)
                p = jnp.exp2(s - m).astype(BF16)
                pv = jnp.dot(p, vb, preferred_element_type=F32)
                ls.append(pv[:, V7X_LANES:])
                acc = pv[:, :V7X_LANES] if acc is None else acc + pv[:, :V7X_LANES]
            out = acc / jnp.where(lane2 < half, ls[0], ls[1])
            out = out.astype(BF16)
            c0 = kh * 2 * V7X_LANES
            ao_ref[pl.ds(r0, BLOCK), c0:c0 + V7X_LANES] = out[:BLOCK]
            ao_ref[pl.ds(r0, BLOCK), c0 + V7X_LANES:c0 + 2 * V7X_LANES] = out[BLOCK:]
        return carry

    lax.fori_loop(0, nblk, block_body, 0, unroll=True)

    f = jnp.dot(ao_ref[...], wo_ref[...], preferred_element_type=F32) + bo_ref[...]
    o_ref[...] = h + _rms(f, post_ref[...])


def _t5_causal_bucket_np(dist):
    dist = np.maximum(dist, 0)
    max_exact = NUM_BUCKETS // 2
    large = max_exact + (
        np.log(np.maximum(dist, 1).astype(np.float32) / np.float32(max_exact))
        / np.float32(math.log(MAX_DISTANCE / max_exact)) * np.float32(NUM_BUCKETS - max_exact)
    ).astype(np.int32)
    large = np.minimum(large, NUM_BUCKETS - 1)
    return np.where(dist < max_exact, dist, large)


def _attn_tables():
    q_loc = np.arange(BLOCK, dtype=np.int32)[:, None] + BLOCK
    s_loc = np.arange(2 * BLOCK, dtype=np.int32)[None, :]
    dist = q_loc - s_loc
    band = (dist >= 0) & (dist < WINDOW)
    bucket = _t5_causal_bucket_np(dist).astype(np.int32)
    first = band & (s_loc >= BLOCK)
    valid = np.stack([np.tile(band, (2, 1)), np.tile(first, (2, 1))])
    assert not valid[:, :, 0].any()
    valid[:, :, 0] = True
    cap = np.where(valid, np.inf, -np.inf).astype(np.float32)
    return jnp.asarray(bucket), jnp.asarray(cap)


def _attn_call(h, seq_len, pre_g, post_g, w_qkv, b_qkv, w_o, b_o, sinks, rel_bias):
    t, d = h.shape
    tm = TOKEN_TILE
    assert seq_len % tm == 0 and tm % BLOCK == 0
    nblk = tm // BLOCK
    bucket, cap = _attn_tables()
    return pl.pallas_call(
        functools.partial(_attn_kernel, seq_len // tm),
        grid=(t // tm,),
        in_specs=[
            _tile_spec(tm, d),
            _const_spec((1, d)),
            _const_spec((1, d)),
            _const_spec(w_qkv.shape),
            _const_spec((1, w_qkv.shape[1])),
            _const_spec(w_o.shape),
            _const_spec((1, d)),
            _const_spec(bucket.shape),
            _const_spec(cap.shape),
            pl.BlockSpec(memory_space=pltpu.SMEM),
            pl.BlockSpec(memory_space=pltpu.SMEM),
        ],
        out_specs=_tile_spec(tm, d),
        out_shape=jax.ShapeDtypeStruct((t, d), F32),
        scratch_shapes=[
            pltpu.VMEM((N_KV_HEADS, nblk, 2 * BLOCK, V7X_LANES), BF16),
            pltpu.VMEM((N_KV_HEADS, 2, nblk, 2 * BLOCK, V7X_LANES), BF16),
            pltpu.VMEM((N_KV_HEADS, 2, nblk, 2 * BLOCK, 2 * V7X_LANES), BF16),
            pltpu.VMEM((N_KV_HEADS, 2, BLOCK, V7X_LANES), BF16),
            pltpu.VMEM((N_KV_HEADS, 2, BLOCK, V7X_LANES), BF16),
            pltpu.VMEM((tm, d), BF16),
            pltpu.VMEM((N_Q_HEADS, BLOCK, 2 * BLOCK), F32),
        ],
        compiler_params=_params(),
        name="attn_mixer",
    )(h, pre_g, post_g, w_qkv, b_qkv, w_o, b_o, bucket, cap, sinks, rel_bias)


def kernel(x, mix_pre_g, mix_post_g, ffn_pre_g, ffn_post_g, conv_w_in, conv_b_in, conv_dw_w, conv_dw_b, conv_ln_g, conv_ln_b, conv_w_out, conv_b_out, attn_w_qkv, attn_b_qkv, attn_w_o, attn_b_o, attn_sinks, rel_bias, ffn_w_gate_up, ffn_w_down):
    b, s, d = x.shape
    depth = mix_pre_g.shape[0]
    h = x.reshape(b * s, d)
    row = lambda v: v.reshape(1, -1)
    w_gu = ffn_w_gate_up.astype(BF16)
    w_d = ffn_w_down.astype(BF16)
    for i in range(depth):
        j = i // 2
        if i % 2 == 0:
            h = _conv_call(
                h, s, row(mix_pre_g[i]), row(mix_post_g[i]),
                conv_w_in[j].astype(BF16), row(conv_b_in[j]), conv_dw_w[j],
                row(conv_dw_b[j]), row(conv_ln_g[j]), row(conv_ln_b[j]),
                conv_w_out[j].astype(BF16), row(conv_b_out[j]))
        else:
            h = _attn_call(
                h, s, row(mix_pre_g[i]), row(mix_post_g[i]),
                attn_w_qkv[j].astype(BF16), row(attn_b_qkv[j]),
                attn_w_o[j].astype(BF16), row(attn_b_o[j]), attn_sinks[j], rel_bias)
        h = _ffn_call(h, i, row(ffn_pre_g[i]), row(ffn_post_g[i]), w_gu, w_d)
    return h.reshape(b, s, d)
```

```python
import functools
import math

import jax
import jax.numpy as jnp
import numpy as np
from jax import lax
from jax.experimental import pallas as pl
from jax.experimental.pallas import tpu as pltpu

D_MODEL = 1024
CONV_WIDTH = 31
HEAD_DIM = 64
N_Q_HEADS = D_MODEL // HEAD_DIM
N_KV_HEADS = 4
WINDOW = 128
BLOCK = 128
NUM_BUCKETS = 32
MAX_DISTANCE = 128
D_FF = ((8 * D_MODEL // 3 + 255) // 256) * 256
EPS = 1e-6
LOG2_E = math.log2(math.e)

V7X_LANES = 128
V7X_SUBLANES = 8
V7X_VMEM_BYTES = 64 * 1024 * 1024

TOKEN_TILE = 1024
FFN_TILE = 1024
FFN_ROWS = 256
CONV_HALO = 32
CONV_SUBTILE = 64
VMEM_LIMIT = V7X_VMEM_BYTES * 7 // 8

F32 = jnp.float32
BF16 = jnp.bfloat16


def _rms(x, g):
    ms = jnp.mean(x * x, axis=-1, keepdims=True)
    return x * lax.rsqrt(ms + EPS) * g


def _sigmoid(x):
    return 1.0 / (1.0 + jnp.exp2(x * -LOG2_E))


def _const_spec(shape):
    zeros = (0,) * len(shape)
    return pl.BlockSpec(shape, lambda i: zeros, pipeline_mode=pl.Buffered(1))


def _layer_spec(stacked, layer):
    zeros = (0,) * (stacked.ndim - 1)
    return pl.BlockSpec((None,) + stacked.shape[1:], lambda i: (layer,) + zeros,
                        pipeline_mode=pl.Buffered(1))


def _tile_spec(tm, d):
    return pl.BlockSpec((tm, d), lambda i: (i, 0))


def _params():
    return pltpu.CompilerParams(
        dimension_semantics=("arbitrary",), vmem_limit_bytes=VMEM_LIMIT)


def _ffn_kernel(h_ref, pre_ref, post_ref, wgu_ref, wd_ref, o_ref):
    tm = h_ref.shape[0]
    for r0 in range(0, tm, FFN_ROWS):
        rows = slice(r0, r0 + FFN_ROWS)
        h = h_ref[rows, :]
        xn = _rms(h, pre_ref[...]).astype(BF16)
        g = jnp.dot(xn, wgu_ref[:, :D_FF], preferred_element_type=F32)
        u = jnp.dot(xn, wgu_ref[:, D_FF:], preferred_element_type=F32)
        a = (g * _sigmoid(g) * u).astype(BF16)
        f = jnp.dot(a, wd_ref[...], preferred_element_type=F32)
        o_ref[rows, :] = h + _rms(f, post_ref[...])


def _ffn_call(h, layer, pre_g, post_g, w_gu, w_d):
    t, d = h.shape
    tm = FFN_TILE
    assert t % tm == 0 and tm % FFN_ROWS == 0
    return pl.pallas_call(
        _ffn_kernel,
        grid=(t // tm,),
        in_specs=[
            _tile_spec(tm, d),
            _const_spec((1, d)),
            _const_spec((1, d)),
            _layer_spec(w_gu, layer),
            _layer_spec(w_d, layer),
        ],
        out_specs=_tile_spec(tm, d),
        out_shape=jax.ShapeDtypeStruct((t, d), F32),
        compiler_params=_params(),
        name="ffn",
    )(h, pre_g, post_g, w_gu, w_d)


def _conv_kernel(tiles_per_seq, h_ref, pre_ref, post_ref, win_ref, bin_ref,
                 dww_ref, dwb_ref, lng_ref, lnb_ref, wout_ref, bout_ref,
                 o_ref, ext_ref, y_ref):
    tm, d = h_ref.shape
    n_lane = d // V7X_LANES
    i = pl.program_id(0)

    @pl.when(i % tiles_per_seq == 0)
    def _():
        ext_ref[:, 0:CONV_HALO, :] = jnp.zeros((n_lane, CONV_HALO, V7X_LANES), F32)

    h = h_ref[...]
    xn = _rms(h, pre_ref[...]).astype(BF16)
    off0 = CONV_HALO - (CONV_WIDTH - 1)
    ts = CONV_SUBTILE

    def chunk_body(cc, carry):
        for r0 in range(0, tm, ts):
            acc = jnp.broadcast_to(dwb_ref[cc], (ts, V7X_LANES))
            for k in range(CONV_WIDTH):
                acc = acc + ext_ref[cc, r0 + off0 + k:r0 + off0 + k + ts, :] * dww_ref[cc, k:k + 1, :]
            y_ref[cc, r0:r0 + ts, :] = acc
        return carry

    wide = 2 * V7X_LANES
    for c0 in range(0, d, wide):
        val = jnp.dot(xn, win_ref[:, c0:c0 + wide], preferred_element_type=F32) + bin_ref[:, c0:c0 + wide]
        gate = jnp.dot(xn, win_ref[:, d + c0:d + c0 + wide],
                       preferred_element_type=F32) + bin_ref[:, d + c0:d + c0 + wide]
        glu = val * _sigmoid(gate)
        for cc in range(c0 // V7X_LANES, (c0 + wide) // V7X_LANES):
            lo = cc * V7X_LANES - c0
            ext_ref[cc, CONV_HALO:CONV_HALO + tm, :] = glu[:, lo:lo + V7X_LANES]
            chunk_body(cc, 0)

    ext_ref[:, 0:CONV_HALO, :] = ext_ref[:, tm:tm + CONV_HALO, :]

    y = jnp.concatenate([y_ref[cc] for cc in range(n_lane)], axis=1)
    mu = jnp.mean(y, axis=-1, keepdims=True)
    yc = y - mu
    var = jnp.mean(yc * yc, axis=-1, keepdims=True)
    z = yc * lax.rsqrt(var + EPS) * lng_ref[...] + lnb_ref[...]
    z = (z * _sigmoid(z)).astype(BF16)
    f = jnp.dot(z, wout_ref[...], preferred_element_type=F32) + bout_ref[...]
    o_ref[...] = h + _rms(f, post_ref[...])


def _conv_call(h, seq_len, pre_g, post_g, w_in, b_in, dw_w, dw_b, ln_g, ln_b,
               w_out, b_out):
    t, d = h.shape
    tm = TOKEN_TILE
    assert seq_len % tm == 0 and CONV_HALO >= CONV_WIDTH - 1
    n_lane = d // V7X_LANES
    dw_w = dw_w.reshape(CONV_WIDTH, n_lane, V7X_LANES).transpose(1, 0, 2)
    dw_b = dw_b.reshape(n_lane, 1, V7X_LANES)
    return pl.pallas_call(
        functools.partial(_conv_kernel, seq_len // tm),
        grid=(t // tm,),
        in_specs=[
            _tile_spec(tm, d),
            _const_spec((1, d)),
            _const_spec((1, d)),
            _const_spec(w_in.shape),
            _const_spec((1, 2 * d)),
            _const_spec(dw_w.shape),
            _const_spec(dw_b.shape),
            _const_spec((1, d)),
            _const_spec((1, d)),
            _const_spec(w_out.shape),
            _const_spec((1, d)),
        ],
        out_specs=_tile_spec(tm, d),
        out_shape=jax.ShapeDtypeStruct((t, d), F32),
        scratch_shapes=[
            pltpu.VMEM((n_lane, tm + CONV_HALO, V7X_LANES), F32),
            pltpu.VMEM((n_lane, tm, V7X_LANES), F32),
        ],
        compiler_params=_params(),
        name="conv_mixer",
    )(h, pre_g, post_g, w_in, b_in, dw_w, dw_b, ln_g, ln_b, w_out, b_out)


def _attn_kernel(tiles_per_seq, h_ref, pre_ref, post_ref, wqkv_ref, bqkv_ref,
                 wo_ref, bo_ref, bucket_ref, cap_ref, sink_ref, relb_ref,
                 o_ref, q_ref, kw_ref, vw_ref, kc_ref, vc_ref, ao_ref, bias_ref):
    tm, d = h_ref.shape
    nblk = tm // BLOCK
    half = V7X_LANES // 2
    i = pl.program_id(0)
    seq_start = i % tiles_per_seq == 0

    @pl.when(i == 0)
    def _():
        vw_ref[:, :, :, :, V7X_LANES:] = jnp.ones(
            (N_KV_HEADS, 2, nblk, 2 * BLOCK, V7X_LANES), BF16)
        bucket = bucket_ref[...]
        key_col = lax.broadcasted_iota(jnp.int32, bucket.shape, 1)

        def head_body(hh, carry):
            tile = jnp.zeros(bucket.shape, F32)
            for b in range(NUM_BUCKETS):
                tile = jnp.where(bucket == b, relb_ref[b, hh], tile)
            bias_ref[hh] = jnp.where(key_col == 0, sink_ref[hh], tile) * LOG2_E
            return carry

        lax.fori_loop(0, N_Q_HEADS, head_body, 0)

    @pl.when(seq_start)
    def _():
        z = jnp.zeros((N_KV_HEADS, 2, BLOCK, V7X_LANES), BF16)
        kc_ref[...] = z
        vc_ref[...] = z

    h = h_ref[...]
    xn = _rms(h, pre_ref[...]).astype(BF16)

    scale = HEAD_DIM ** -0.5 * LOG2_E
    qkv = jnp.dot(xn, wqkv_ref[...], preferred_element_type=F32) + bqkv_ref[...]
    for kh in range(N_KV_HEADS):
        for pr in range(2):
            c0 = (2 * kh + pr) * V7X_LANES
            qp = (qkv[:, c0:c0 + V7X_LANES] * scale).astype(BF16)
            for j in range(nblk):
                q_ref[kh, j, pr * BLOCK:(pr + 1) * BLOCK, :] = qp[j * BLOCK:(j + 1) * BLOCK, :]

    lane = lax.broadcasted_iota(jnp.int32, (tm, V7X_LANES), 1)
    low = lane < half
    row = lax.broadcasted_iota(jnp.int32, (tm, V7X_LANES), 0)
    block_start = row % BLOCK == 0
    for which, win_ref, carry_ref in ((0, kw_ref, kc_ref), (1, vw_ref, vc_ref)):
        base = N_Q_HEADS * HEAD_DIM + which * N_KV_HEADS * HEAD_DIM
        for col in range(N_KV_HEADS // 2):
            c0 = base + col * V7X_LANES
            x = qkv[:, c0:c0 + V7X_LANES]
            xs = pltpu.roll(x, half, axis=1)
            zero = jnp.zeros_like(x)
            variants = {(2 * col, 0): jnp.where(low, x, zero),
                        (2 * col, 1): jnp.where(low, zero, xs),
                        (2 * col + 1, 0): jnp.where(low, xs, zero),
                        (2 * col + 1, 1): jnp.where(low, zero, x)}
            for (kh, var), val in variants.items():
                cur = val.astype(BF16)
                prv = jnp.where(block_start, zero, val).astype(BF16)
                win_ref[kh, var, 0, 0:BLOCK, 0:V7X_LANES] = carry_ref[kh, var]
                for j in range(nblk):
                    blk = slice(j * BLOCK, (j + 1) * BLOCK)
                    win_ref[kh, var, j, BLOCK:2 * BLOCK, 0:V7X_LANES] = cur[blk]
                    if j + 1 < nblk:
                        win_ref[kh, var, j + 1, 0:BLOCK, 0:V7X_LANES] = prv[blk]
                carry_ref[kh, var] = prv[(nblk - 1) * BLOCK:]

    lane2 = lax.broadcasted_iota(jnp.int32, (2 * BLOCK, V7X_LANES), 1)

    def block_body(j, carry):
        r0 = pl.multiple_of(j * BLOCK, BLOCK)
        first = jnp.logical_and(seq_start, j == 0).astype(jnp.int32)
        for kh in range(N_KV_HEADS):
            q2 = q_ref[kh, j]
            ls = []
            acc = None
            for eo in range(2):
                kb = kw_ref[kh, eo, j]
                vb = vw_ref[kh, eo, j]
                s = lax.dot_general(q2, kb, (((1,), (1,)), ((), ())),
                                    preferred_element_type=F32)
                h_top = 4 * kh + eo
                bias = jnp.concatenate([bias_ref[h_top], bias_ref[h_top + 2]], axis=0)
                s = jnp.minimum(s + bias, cap_ref[first])
                m = jnp.max(s, axis=-1, keepdims=True)
                p = jnp.exp2(s - m).astype(BF16)
                pv = jnp.dot(p, vb, preferred_element_type=F32)
                ls.append(pv[:, V7X_LANES:])
                acc = pv[:, :V7X_LANES] if acc is None else acc + pv[:, :V7X_LANES]
            out = acc / jnp.where(lane2 < half, ls[0], ls[1])
            out = out.astype(BF16)
            c0 = kh * 2 * V7X_LANES
            ao_ref[pl.ds(r0, BLOCK), c0:c0 + V7X_LANES] = out[:BLOCK]
            ao_ref[pl.ds(r0, BLOCK), c0 + V7X_LANES:c0 + 2 * V7X_LANES] = out[BLOCK:]
        return carry

    lax.fori_loop(0, nblk, block_body, 0, unroll=True)

    f = jnp.dot(ao_ref[...], wo_ref[...], preferred_element_type=F32) + bo_ref[...]
    o_ref[...] = h + _rms(f, post_ref[...])


def _t5_causal_bucket_np(dist):
    dist = np.maximum(dist, 0)
    max_exact = NUM_BUCKETS // 2
    large = max_exact + (
        np.log(np.maximum(dist, 1).astype(np.float32) / np.float32(max_exact))
        / np.float32(math.log(MAX_DISTANCE / max_exact)) * np.float32(NUM_BUCKETS - max_exact)
    ).astype(np.int32)
    large = np.minimum(large, NUM_BUCKETS - 1)
    return np.where(dist < max_exact, dist, large)


def _attn_tables():
    q_loc = np.arange(BLOCK, dtype=np.int32)[:, None] + BLOCK
    s_loc = np.arange(2 * BLOCK, dtype=np.int32)[None, :]
    dist = q_loc - s_loc
    band = (dist >= 0) & (dist < WINDOW)
    bucket = _t5_causal_bucket_np(dist).astype(np.int32)
    first = band & (s_loc >= BLOCK)
    valid = np.stack([np.tile(band, (2, 1)), np.tile(first, (2, 1))])
    assert not valid[:, :, 0].any()
    valid[:, :, 0] = True
    cap = np.where(valid, np.inf, -np.inf).astype(np.float32)
    return jnp.asarray(bucket), jnp.asarray(cap)


def _attn_call(h, seq_len, pre_g, post_g, w_qkv, b_qkv, w_o, b_o, sinks, rel_bias):
    t, d = h.shape
    tm = TOKEN_TILE
    assert seq_len % tm == 0 and tm % BLOCK == 0
    nblk = tm // BLOCK
    bucket, cap = _attn_tables()
    return pl.pallas_call(
        functools.partial(_attn_kernel, seq_len // tm),
        grid=(t // tm,),
        in_specs=[
            _tile_spec(tm, d),
            _const_spec((1, d)),
            _const_spec((1, d)),
            _const_spec(w_qkv.shape),
            _const_spec((1, w_qkv.shape[1])),
            _const_spec(w_o.shape),
            _const_spec((1, d)),
            _const_spec(bucket.shape),
            _const_spec(cap.shape),
            pl.BlockSpec(memory_space=pltpu.SMEM),
            pl.BlockSpec(memory_space=pltpu.SMEM),
        ],
        out_specs=_tile_spec(tm, d),
        out_shape=jax.ShapeDtypeStruct((t, d), F32),
        scratch_shapes=[
            pltpu.VMEM((N_KV_HEADS, nblk, 2 * BLOCK, V7X_LANES), BF16),
            pltpu.VMEM((N_KV_HEADS, 2, nblk, 2 * BLOCK, V7X_LANES), BF16),
            pltpu.VMEM((N_KV_HEADS, 2, nblk, 2 * BLOCK, 2 * V7X_LANES), BF16),
            pltpu.VMEM((N_KV_HEADS, 2, BLOCK, V7X_LANES), BF16),
            pltpu.VMEM((N_KV_HEADS, 2, BLOCK, V7X_LANES), BF16),
            pltpu.VMEM((tm, d), BF16),
            pltpu.VMEM((N_Q_HEADS, BLOCK, 2 * BLOCK), F32),
        ],
        compiler_params=_params(),
        name="attn_mixer",
    )(h, pre_g, post_g, w_qkv, b_qkv, w_o, b_o, bucket, cap, sinks, rel_bias)


def kernel(x, mix_pre_g, mix_post_g, ffn_pre_g, ffn_post_g, conv_w_in, conv_b_in, conv_dw_w, conv_dw_b, conv_ln_g, conv_ln_b, conv_w_out, conv_b_out, attn_w_qkv, attn_b_qkv, attn_w_o, attn_b_o, attn_sinks, rel_bias, ffn_w_gate_up, ffn_w_down):
    b, s, d = x.shape
    depth = mix_pre_g.shape[0]
    h = x.reshape(b * s, d)
    row = lambda v: v.reshape(1, -1)
    w_gu = ffn_w_gate_up.astype(BF16)
    w_d = ffn_w_down.astype(BF16)
    for i in range(depth):
        j = i // 2
        if i % 2 == 0:
            h = _conv_call(
                h, s, row(mix_pre_g[i]), row(mix_post_g[i]),
                conv_w_in[j].astype(BF16), row(conv_b_in[j]), conv_dw_w[j],
                row(conv_dw_b[j]), row(conv_ln_g[j]), row(conv_ln_b[j]),
                conv_w_out[j].astype(BF16), row(conv_b_out[j]))
        else:
            h = _attn_call(
                h, s, row(mix_pre_g[i]), row(mix_post_g[i]),
                attn_w_qkv[j].astype(BF16), row(attn_b_qkv[j]),
                attn_w_o[j].astype(BF16), row(attn_b_o[j]), attn_sinks[j], rel_bias)
        h = _ffn_call(h, i, row(ffn_pre_g[i]), row(ffn_post_g[i]), w_gu, w_d)
    return h.reshape(b, s, d)
```

```python
import functools
import math

import jax
import jax.numpy as jnp
import numpy as np
from jax import lax
from jax.experimental import pallas as pl
from jax.experimental.pallas import tpu as pltpu

D_MODEL = 1024
CONV_WIDTH = 31
HEAD_DIM = 64
N_Q_HEADS = D_MODEL // HEAD_DIM
N_KV_HEADS = 4
WINDOW = 128
BLOCK = 128
NUM_BUCKETS = 32
MAX_DISTANCE = 128
D_FF = ((8 * D_MODEL // 3 + 255) // 256) * 256
EPS = 1e-6
LOG2_E = math.log2(math.e)

V7X_LANES = 128
V7X_SUBLANES = 8
V7X_VMEM_BYTES = 64 * 1024 * 1024

TOKEN_TILE = 1024
FFN_TILE = 1024
FFN_ROWS = 256
CONV_HALO = 32
CONV_SUBTILE = 64
VMEM_LIMIT = V7X_VMEM_BYTES * 7 // 8

F32 = jnp.float32
BF16 = jnp.bfloat16


def _rms(x, g):
    ms = jnp.mean(x * x, axis=-1, keepdims=True)
    return x * lax.rsqrt(ms + EPS) * g


def _sigmoid(x):
    return 1.0 / (1.0 + jnp.exp2(x * -LOG2_E))


def _const_spec(shape):
    zeros = (0,) * len(shape)
    return pl.BlockSpec(shape, lambda i: zeros, pipeline_mode=pl.Buffered(1))


def _layer_spec(stacked, layer):
    zeros = (0,) * (stacked.ndim - 1)
    return pl.BlockSpec((None,) + stacked.shape[1:], lambda i: (layer,) + zeros,
                        pipeline_mode=pl.Buffered(1))


def _tile_spec(tm, d):
    return pl.BlockSpec((tm, d), lambda i: (i, 0))


def _params():
    return pltpu.CompilerParams(
        dimension_semantics=("arbitrary",), vmem_limit_bytes=VMEM_LIMIT)


def _ffn_kernel(h_ref, pre_ref, post_ref, wgu_ref, wd_ref, o_ref):
    tm = h_ref.shape[0]
    for r0 in range(0, tm, FFN_ROWS):
        rows = slice(r0, r0 + FFN_ROWS)
        h = h_ref[rows, :]
        xn = _rms(h, pre_ref[...]).astype(BF16)
        g = jnp.dot(xn, wgu_ref[:, :D_FF], preferred_element_type=F32)
        u = jnp.dot(xn, wgu_ref[:, D_FF:], preferred_element_type=F32)
        a = (g * _sigmoid(g) * u).astype(BF16)
        f = jnp.dot(a, wd_ref[...], preferred_element_type=F32)
        o_ref[rows, :] = h + _rms(f, post_ref[...])


def _ffn_call(h, layer, pre_g, post_g, w_gu, w_d):
    t, d = h.shape
    tm = FFN_TILE
    assert t % tm == 0 and tm % FFN_ROWS == 0
    return pl.pallas_call(
        _ffn_kernel,
        grid=(t // tm,),
        in_specs=[
            _tile_spec(tm, d),
            _const_spec((1, d)),
            _const_spec((1, d)),
            _layer_spec(w_gu, layer),
            _layer_spec(w_d, layer),
        ],
        out_specs=_tile_spec(tm, d),
        out_shape=jax.ShapeDtypeStruct((t, d), F32),
        compiler_params=_params(),
        name="ffn",
    )(h, pre_g, post_g, w_gu, w_d)


def _conv_kernel(tiles_per_seq, h_ref, pre_ref, post_ref, win_ref, bin_ref,
                 dww_ref, dwb_ref, lng_ref, lnb_ref, wout_ref, bout_ref,
                 o_ref, ext_ref, y_ref):
    tm, d = h_ref.shape
    n_lane = d // V7X_LANES
    i = pl.program_id(0)

    @pl.when(i % tiles_per_seq == 0)
    def _():
        ext_ref[:, 0:CONV_HALO, :] = jnp.zeros((n_lane, CONV_HALO, V7X_LANES), F32)

    h = h_ref[...]
    xn = _rms(h, pre_ref[...]).astype(BF16)
    off0 = CONV_HALO - (CONV_WIDTH - 1)
    ts = CONV_SUBTILE

    def chunk_body(cc, carry):
        for r0 in range(0, tm, ts):
            acc = jnp.broadcast_to(dwb_ref[cc], (ts, V7X_LANES))
            for k in range(CONV_WIDTH):
                acc = acc + ext_ref[cc, r0 + off0 + k:r0 + off0 + k + ts, :] * dww_ref[cc, k:k + 1, :]
            y_ref[cc, r0:r0 + ts, :] = acc
        return carry

    widths = [V7X_LANES, V7X_LANES] + [2 * V7X_LANES] * ((d - 2 * V7X_LANES) // (2 * V7X_LANES))
    starts = [sum(widths[:n]) for n in range(len(widths))]
    for c0, wide in zip(starts, widths):
        val = jnp.dot(xn, win_ref[:, c0:c0 + wide], preferred_element_type=F32) + bin_ref[:, c0:c0 + wide]
        gate = jnp.dot(xn, win_ref[:, d + c0:d + c0 + wide],
                       preferred_element_type=F32) + bin_ref[:, d + c0:d + c0 + wide]
        glu = val * _sigmoid(gate)
        for cc in range(c0 // V7X_LANES, (c0 + wide) // V7X_LANES):
            lo = cc * V7X_LANES - c0
            ext_ref[cc, CONV_HALO:CONV_HALO + tm, :] = glu[:, lo:lo + V7X_LANES]
            chunk_body(cc, 0)

    ext_ref[:, 0:CONV_HALO, :] = ext_ref[:, tm:tm + CONV_HALO, :]

    y = jnp.concatenate([y_ref[cc] for cc in range(n_lane)], axis=1)
    mu = jnp.mean(y, axis=-1, keepdims=True)
    yc = y - mu
    var = jnp.mean(yc * yc, axis=-1, keepdims=True)
    z = yc * lax.rsqrt(var + EPS) * lng_ref[...] + lnb_ref[...]
    z = (z * _sigmoid(z)).astype(BF16)
    f = jnp.dot(z, wout_ref[...], preferred_element_type=F32) + bout_ref[...]
    o_ref[...] = h + _rms(f, post_ref[...])


def _conv_call(h, seq_len, pre_g, post_g, w_in, b_in, dw_w, dw_b, ln_g, ln_b,
               w_out, b_out):
    t, d = h.shape
    tm = TOKEN_TILE
    assert seq_len % tm == 0 and CONV_HALO >= CONV_WIDTH - 1
    n_lane = d // V7X_LANES
    dw_w = dw_w.reshape(CONV_WIDTH, n_lane, V7X_LANES).transpose(1, 0, 2)
    dw_b = dw_b.reshape(n_lane, 1, V7X_LANES)
    return pl.pallas_call(
        functools.partial(_conv_kernel, seq_len // tm),
        grid=(t // tm,),
        in_specs=[
            _tile_spec(tm, d),
            _const_spec((1, d)),
            _const_spec((1, d)),
            _const_spec(w_in.shape),
            _const_spec((1, 2 * d)),
            _const_spec(dw_w.shape),
            _const_spec(dw_b.shape),
            _const_spec((1, d)),
            _const_spec((1, d)),
            _const_spec(w_out.shape),
            _const_spec((1, d)),
        ],
        out_specs=_tile_spec(tm, d),
        out_shape=jax.ShapeDtypeStruct((t, d), F32),
        scratch_shapes=[
            pltpu.VMEM((n_lane, tm + CONV_HALO, V7X_LANES), F32),
            pltpu.VMEM((n_lane, tm, V7X_LANES), F32),
        ],
        compiler_params=_params(),
        name="conv_mixer",
    )(h, pre_g, post_g, w_in, b_in, dw_w, dw_b, ln_g, ln_b, w_out, b_out)


def _attn_kernel(tiles_per_seq, h_ref, pre_ref, post_ref, wqkv_ref, bqkv_ref,
                 wo_ref, bo_ref, bucket_ref, cap_ref, sink_ref, relb_ref,
                 o_ref, q_ref, kw_ref, vw_ref, kc_ref, vc_ref, ao_ref, bias_ref):
    tm, d = h_ref.shape
    nblk = tm // BLOCK
    half = V7X_LANES // 2
    i = pl.program_id(0)
    seq_start = i % tiles_per_seq == 0

    @pl.when(i == 0)
    def _():
        vw_ref[:, :, :, :, V7X_LANES:] = jnp.ones(
            (N_KV_HEADS, 2, nblk, 2 * BLOCK, V7X_LANES), BF16)
        bucket = bucket_ref[...]
        key_col = lax.broadcasted_iota(jnp.int32, bucket.shape, 1)

        def head_body(hh, carry):
            tile = jnp.zeros(bucket.shape, F32)
            for b in range(NUM_BUCKETS):
                tile = jnp.where(bucket == b, relb_ref[b, hh], tile)
            bias_ref[hh] = jnp.where(key_col == 0, sink_ref[hh], tile) * LOG2_E
            return carry

        lax.fori_loop(0, N_Q_HEADS, head_body, 0)

    @pl.when(seq_start)
    def _():
        z = jnp.zeros((N_KV_HEADS, 2, BLOCK, V7X_LANES), BF16)
        kc_ref[...] = z
        vc_ref[...] = z

    h = h_ref[...]
    xn = _rms(h, pre_ref[...]).astype(BF16)

    scale = HEAD_DIM ** -0.5 * LOG2_E
    qkv = jnp.dot(xn, wqkv_ref[...], preferred_element_type=F32) + bqkv_ref[...]
    for kh in range(N_KV_HEADS):
        for pr in range(2):
            c0 = (2 * kh + pr) * V7X_LANES
            qp = (qkv[:, c0:c0 + V7X_LANES] * scale).astype(BF16)
            for j in range(nblk):
                q_ref[kh, j, pr * BLOCK:(pr + 1) * BLOCK, :] = qp[j * BLOCK:(j + 1) * BLOCK, :]

    lane = lax.broadcasted_iota(jnp.int32, (tm, V7X_LANES), 1)
    low = lane < half
    row = lax.broadcasted_iota(jnp.int32, (tm, V7X_LANES), 0)
    block_start = row % BLOCK == 0
    for which, win_ref, carry_ref in ((0, kw_ref, kc_ref), (1, vw_ref, vc_ref)):
        base = N_Q_HEADS * HEAD_DIM + which * N_KV_HEADS * HEAD_DIM
        for col in range(N_KV_HEADS // 2):
            c0 = base + col * V7X_LANES
            x = qkv[:, c0:c0 + V7X_LANES]
            xs = pltpu.roll(x, half, axis=1)
            zero = jnp.zeros_like(x)
            variants = {(2 * col, 0): jnp.where(low, x, zero),
                        (2 * col, 1): jnp.where(low, zero, xs),
                        (2 * col + 1, 0): jnp.where(low, xs, zero),
                        (2 * col + 1, 1): jnp.where(low, zero, x)}
            for (kh, var), val in variants.items():
                cur = val.astype(BF16)
                prv = jnp.where(block_start, zero, val).astype(BF16)
                win_ref[kh, var, 0, 0:BLOCK, 0:V7X_LANES] = carry_ref[kh, var]
                for j in range(nblk):
                    blk = slice(j * BLOCK, (j + 1) * BLOCK)
                    win_ref[kh, var, j, BLOCK:2 * BLOCK, 0:V7X_LANES] = cur[blk]
                    if j + 1 < nblk:
                        win_ref[kh, var, j + 1, 0:BLOCK, 0:V7X_LANES] = prv[blk]
                carry_ref[kh, var] = prv[(nblk - 1) * BLOCK:]

    lane2 = lax.broadcasted_iota(jnp.int32, (2 * BLOCK, V7X_LANES), 1)

    def block_body(j, carry):
        r0 = pl.multiple_of(j * BLOCK, BLOCK)
        first = jnp.logical_and(seq_start, j == 0).astype(jnp.int32)
        for kh in range(N_KV_HEADS):
            q2 = q_ref[kh, j]
            ls = []
            acc = None
            for eo in range(2):
                kb = kw_ref[kh, eo, j]
                vb = vw_ref[kh, eo, j]
                s = lax.dot_general(q2, kb, (((1,), (1,)), ((), ())),
                                    preferred_element_type=F32)
                h_top = 4 * kh + eo
                bias = jnp.concatenate([bias_ref[h_top], bias_ref[h_top + 2]], axis=0)
                s = jnp.minimum(s + bias, cap_ref[first])
                m = jnp.max(s, axis=-1, keepdims=True)
                p = jnp.exp2(s - m).astype(BF16)
                pv = jnp.dot(p, vb, preferred_element_type=F32)
                ls.append(pv[:, V7X_LANES:])
                acc = pv[:, :V7X_LANES] if acc is None else acc + pv[:, :V7X_LANES]
            out = acc / jnp.where(lane2 < half, ls[0], ls[1])
            out = out.astype(BF16)
            c0 = kh * 2 * V7X_LANES
            ao_ref[pl.ds(r0, BLOCK), c0:c0 + V7X_LANES] = out[:BLOCK]
            ao_ref[pl.ds(r0, BLOCK), c0 + V7X_LANES:c0 + 2 * V7X_LANES] = out[BLOCK:]
        return carry

    lax.fori_loop(0, nblk, block_body, 0, unroll=True)

    f = jnp.dot(ao_ref[...], wo_ref[...], preferred_element_type=F32) + bo_ref[...]
    o_ref[...] = h + _rms(f, post_ref[...])


def _t5_causal_bucket_np(dist):
    dist = np.maximum(dist, 0)
    max_exact = NUM_BUCKETS // 2
    large = max_exact + (
        np.log(np.maximum(dist, 1).astype(np.float32) / np.float32(max_exact))
        / np.float32(math.log(MAX_DISTANCE / max_exact)) * np.float32(NUM_BUCKETS - max_exact)
    ).astype(np.int32)
    large = np.minimum(large, NUM_BUCKETS - 1)
    return np.where(dist < max_exact, dist, large)


def _attn_tables():
    q_loc = np.arange(BLOCK, dtype=np.int32)[:, None] + BLOCK
    s_loc = np.arange(2 * BLOCK, dtype=np.int32)[None, :]
    dist = q_loc - s_loc
    band = (dist >= 0) & (dist < WINDOW)
    bucket = _t5_causal_bucket_np(dist).astype(np.int32)
    first = band & (s_loc >= BLOCK)
    valid = np.stack([np.tile(band, (2, 1)), np.tile(first, (2, 1))])
    assert not valid[:, :, 0].any()
    valid[:, :, 0] = True
    cap = np.where(valid, np.inf, -np.inf).astype(np.float32)
    return jnp.asarray(bucket), jnp.asarray(cap)


def _attn_call(h, seq_len, pre_g, post_g, w_qkv, b_qkv, w_o, b_o, sinks, rel_bias):
    t, d = h.shape
    tm = TOKEN_TILE
    assert seq_len % tm == 0 and tm % BLOCK == 0
    nblk = tm // BLOCK
    bucket, cap = _attn_tables()
    return pl.pallas_call(
        functools.partial(_attn_kernel, seq_len // tm),
        grid=(t // tm,),
        in_specs=[
            _tile_spec(tm, d),
            _const_spec((1, d)),
            _const_spec((1, d)),
            _const_spec(w_qkv.shape),
            _const_spec((1, w_qkv.shape[1])),
            _const_spec(w_o.shape),
            _const_spec((1, d)),
            _const_spec(bucket.shape),
            _const_spec(cap.shape),
            pl.BlockSpec(memory_space=pltpu.SMEM),
            pl.BlockSpec(memory_space=pltpu.SMEM),
        ],
        out_specs=_tile_spec(tm, d),
        out_shape=jax.ShapeDtypeStruct((t, d), F32),
        scratch_shapes=[
            pltpu.VMEM((N_KV_HEADS, nblk, 2 * BLOCK, V7X_LANES), BF16),
            pltpu.VMEM((N_KV_HEADS, 2, nblk, 2 * BLOCK, V7X_LANES), BF16),
            pltpu.VMEM((N_KV_HEADS, 2, nblk, 2 * BLOCK, 2 * V7X_LANES), BF16),
            pltpu.VMEM((N_KV_HEADS, 2, BLOCK, V7X_LANES), BF16),
            pltpu.VMEM((N_KV_HEADS, 2, BLOCK, V7X_LANES), BF16),
            pltpu.VMEM((tm, d), BF16),
            pltpu.VMEM((N_Q_HEADS, BLOCK, 2 * BLOCK), F32),
        ],
        compiler_params=_params(),
        name="attn_mixer",
    )(h, pre_g, post_g, w_qkv, b_qkv, w_o, b_o, bucket, cap, sinks, rel_bias)


def kernel(x, mix_pre_g, mix_post_g, ffn_pre_g, ffn_post_g, conv_w_in, conv_b_in, conv_dw_w, conv_dw_b, conv_ln_g, conv_ln_b, conv_w_out, conv_b_out, attn_w_qkv, attn_b_qkv, attn_w_o, attn_b_o, attn_sinks, rel_bias, ffn_w_gate_up, ffn_w_down):
    b, s, d = x.shape
    depth = mix_pre_g.shape[0]
    h = x.reshape(b * s, d)
    row = lambda v: v.reshape(1, -1)
    w_gu = ffn_w_gate_up.astype(BF16)
    w_d = ffn_w_down.astype(BF16)
    for i in range(depth):
        j = i // 2
        if i % 2 == 0:
            h = _conv_call(
                h, s, row(mix_pre_g[i]), row(mix_post_g[i]),
                conv_w_in[j].astype(BF16), row(conv_b_in[j]), conv_dw_w[j],
                row(conv_dw_b[j]), row(conv_ln_g[j]), row(conv_ln_b[j]),
                conv_w_out[j].astype(BF16), row(conv_b_out[j]))
        else:
            h = _attn_call(
                h, s, row(mix_pre_g[i]), row(mix_post_g[i]),
                attn_w_qkv[j].astype(BF16), row(attn_b_qkv[j]),
                attn_w_o[j].astype(BF16), row(attn_b_o[j]), attn_sinks[j], rel_bias)
        h = _ffn_call(h, i, row(ffn_pre_g[i]), row(ffn_post_g[i]), w_gu, w_d)
    return h.reshape(b, s, d)
```

```python
import functools
import math

import jax
import jax.numpy as jnp
import numpy as np
from jax import lax
from jax.experimental import pallas as pl
from jax.experimental.pallas import tpu as pltpu

D_MODEL = 1024
CONV_WIDTH = 31
HEAD_DIM = 64
N_Q_HEADS = D_MODEL // HEAD_DIM
N_KV_HEADS = 4
WINDOW = 128
BLOCK = 128
NUM_BUCKETS = 32
MAX_DISTANCE = 128
D_FF = ((8 * D_MODEL // 3 + 255) // 256) * 256
EPS = 1e-6
LOG2_E = math.log2(math.e)

V7X_LANES = 128
V7X_SUBLANES = 8
V7X_VMEM_BYTES = 64 * 1024 * 1024

TOKEN_TILE = 1024
FFN_TILE = 2048
FFN_ROWS = 256
CONV_HALO = 32
CONV_SUBTILE = 64
VMEM_LIMIT = V7X_VMEM_BYTES * 7 // 8

F32 = jnp.float32
BF16 = jnp.bfloat16


def _rms(x, g):
    ms = jnp.mean(x * x, axis=-1, keepdims=True)
    return x * lax.rsqrt(ms + EPS) * g


def _sigmoid(x):
    return 1.0 / (1.0 + jnp.exp2(x * -LOG2_E))


def _const_spec(shape):
    zeros = (0,) * len(shape)
    return pl.BlockSpec(shape, lambda i: zeros, pipeline_mode=pl.Buffered(1))


def _layer_spec(stacked, layer):
    zeros = (0,) * (stacked.ndim - 1)
    return pl.BlockSpec((None,) + stacked.shape[1:], lambda i: (layer,) + zeros,
                        pipeline_mode=pl.Buffered(1))


def _tile_spec(tm, d):
    return pl.BlockSpec((tm, d), lambda i: (i, 0))


def _params():
    return pltpu.CompilerParams(
        dimension_semantics=("arbitrary",), vmem_limit_bytes=VMEM_LIMIT)


def _ffn_kernel(h_ref, pre_ref, post_ref, wgu_ref, wd_ref, o_ref):
    tm = h_ref.shape[0]
    for r0 in range(0, tm, FFN_ROWS):
        rows = slice(r0, r0 + FFN_ROWS)
        h = h_ref[rows, :]
        xn = _rms(h, pre_ref[...]).astype(BF16)
        g = jnp.dot(xn, wgu_ref[:, :D_FF], preferred_element_type=F32)
        u = jnp.dot(xn, wgu_ref[:, D_FF:], preferred_element_type=F32)
        a = (g * _sigmoid(g) * u).astype(BF16)
        f = jnp.dot(a, wd_ref[...], preferred_element_type=F32)
        o_ref[rows, :] = h + _rms(f, post_ref[...])


def _ffn_call(h, layer, pre_g, post_g, w_gu, w_d):
    t, d = h.shape
    tm = FFN_TILE
    assert t % tm == 0 and tm % FFN_ROWS == 0
    return pl.pallas_call(
        _ffn_kernel,
        grid=(t // tm,),
        in_specs=[
            _tile_spec(tm, d),
            _const_spec((1, d)),
            _const_spec((1, d)),
            _layer_spec(w_gu, layer),
            _layer_spec(w_d, layer),
        ],
        out_specs=_tile_spec(tm, d),
        out_shape=jax.ShapeDtypeStruct((t, d), F32),
        compiler_params=_params(),
        name="ffn",
    )(h, pre_g, post_g, w_gu, w_d)


def _conv_kernel(tiles_per_seq, h_ref, pre_ref, post_ref, win_ref, bin_ref,
                 dww_ref, dwb_ref, lng_ref, lnb_ref, wout_ref, bout_ref,
                 o_ref, ext_ref, y_ref):
    tm, d = h_ref.shape
    n_lane = d // V7X_LANES
    i = pl.program_id(0)

    @pl.when(i % tiles_per_seq == 0)
    def _():
        ext_ref[:, 0:CONV_HALO, :] = jnp.zeros((n_lane, CONV_HALO, V7X_LANES), F32)

    h = h_ref[...]
    xn = _rms(h, pre_ref[...]).astype(BF16)
    off0 = CONV_HALO - (CONV_WIDTH - 1)
    ts = CONV_SUBTILE

    def chunk_body(cc, carry):
        for r0 in range(0, tm, ts):
            acc = jnp.broadcast_to(dwb_ref[cc], (ts, V7X_LANES))
            for k in range(CONV_WIDTH):
                acc = acc + ext_ref[cc, r0 + off0 + k:r0 + off0 + k + ts, :] * dww_ref[cc, k:k + 1, :]
            y_ref[cc, r0:r0 + ts, :] = acc
        return carry

    wide = 2 * V7X_LANES
    for c0 in range(0, d, wide):
        val = jnp.dot(xn, win_ref[:, c0:c0 + wide], preferred_element_type=F32) + bin_ref[:, c0:c0 + wide]
        gate = jnp.dot(xn, win_ref[:, d + c0:d + c0 + wide],
                       preferred_element_type=F32) + bin_ref[:, d + c0:d + c0 + wide]
        glu = val * _sigmoid(gate)
        for cc in range(c0 // V7X_LANES, (c0 + wide) // V7X_LANES):
            lo = cc * V7X_LANES - c0
            ext_ref[cc, CONV_HALO:CONV_HALO + tm, :] = glu[:, lo:lo + V7X_LANES]
            chunk_body(cc, 0)

    ext_ref[:, 0:CONV_HALO, :] = ext_ref[:, tm:tm + CONV_HALO, :]

    y = jnp.concatenate([y_ref[cc] for cc in range(n_lane)], axis=1)
    mu = jnp.mean(y, axis=-1, keepdims=True)
    yc = y - mu
    var = jnp.mean(yc * yc, axis=-1, keepdims=True)
    z = yc * lax.rsqrt(var + EPS) * lng_ref[...] + lnb_ref[...]
    z = (z * _sigmoid(z)).astype(BF16)
    f = jnp.dot(z, wout_ref[...], preferred_element_type=F32) + bout_ref[...]
    o_ref[...] = h + _rms(f, post_ref[...])


def _conv_call(h, seq_len, pre_g, post_g, w_in, b_in, dw_w, dw_b, ln_g, ln_b,
               w_out, b_out):
    t, d = h.shape
    tm = TOKEN_TILE
    assert seq_len % tm == 0 and CONV_HALO >= CONV_WIDTH - 1
    n_lane = d // V7X_LANES
    dw_w = dw_w.reshape(CONV_WIDTH, n_lane, V7X_LANES).transpose(1, 0, 2)
    dw_b = dw_b.reshape(n_lane, 1, V7X_LANES)
    return pl.pallas_call(
        functools.partial(_conv_kernel, seq_len // tm),
        grid=(t // tm,),
        in_specs=[
            _tile_spec(tm, d),
            _const_spec((1, d)),
            _const_spec((1, d)),
            _const_spec(w_in.shape),
            _const_spec((1, 2 * d)),
            _const_spec(dw_w.shape),
            _const_spec(dw_b.shape),
            _const_spec((1, d)),
            _const_spec((1, d)),
            _const_spec(w_out.shape),
            _const_spec((1, d)),
        ],
        out_specs=_tile_spec(tm, d),
        out_shape=jax.ShapeDtypeStruct((t, d), F32),
        scratch_shapes=[
            pltpu.VMEM((n_lane, tm + CONV_HALO, V7X_LANES), F32),
            pltpu.VMEM((n_lane, tm, V7X_LANES), F32),
        ],
        compiler_params=_params(),
        name="conv_mixer",
    )(h, pre_g, post_g, w_in, b_in, dw_w, dw_b, ln_g, ln_b, w_out, b_out)


def _attn_kernel(tiles_per_seq, h_ref, pre_ref, post_ref, wqkv_ref, bqkv_ref,
                 wo_ref, bo_ref, bucket_ref, cap_ref, sink_ref, relb_ref,
                 o_ref, q_ref, kw_ref, vw_ref, kc_ref, vc_ref, ao_ref, bias_ref):
    tm, d = h_ref.shape
    nblk = tm // BLOCK
    half = V7X_LANES // 2
    i = pl.program_id(0)
    seq_start = i % tiles_per_seq == 0

    @pl.when(i == 0)
    def _():
        vw_ref[:, :, :, :, V7X_LANES:] = jnp.ones(
            (N_KV_HEADS, 2, nblk, 2 * BLOCK, V7X_LANES), BF16)
        bucket = bucket_ref[...]
        key_col = lax.broadcasted_iota(jnp.int32, bucket.shape, 1)

        def head_body(hh, carry):
            tile = jnp.zeros(bucket.shape, F32)
            for b in range(NUM_BUCKETS):
                tile = jnp.where(bucket == b, relb_ref[b, hh], tile)
            bias_ref[hh] = jnp.where(key_col == 0, sink_ref[hh], tile) * LOG2_E
            return carry

        lax.fori_loop(0, N_Q_HEADS, head_body, 0)

    @pl.when(seq_start)
    def _():
        z = jnp.zeros((N_KV_HEADS, 2, BLOCK, V7X_LANES), BF16)
        kc_ref[...] = z
        vc_ref[...] = z

    h = h_ref[...]
    xn = _rms(h, pre_ref[...]).astype(BF16)

    scale = HEAD_DIM ** -0.5 * LOG2_E
    qkv = jnp.dot(xn, wqkv_ref[...], preferred_element_type=F32) + bqkv_ref[...]
    for kh in range(N_KV_HEADS):
        for pr in range(2):
            c0 = (2 * kh + pr) * V7X_LANES
            qp = (qkv[:, c0:c0 + V7X_LANES] * scale).astype(BF16)
            for j in range(nblk):
                q_ref[kh, j, pr * BLOCK:(pr + 1) * BLOCK, :] = qp[j * BLOCK:(j + 1) * BLOCK, :]

    lane = lax.broadcasted_iota(jnp.int32, (tm, V7X_LANES), 1)
    low = lane < half
    row = lax.broadcasted_iota(jnp.int32, (tm, V7X_LANES), 0)
    block_start = row % BLOCK == 0
    for which, win_ref, carry_ref in ((0, kw_ref, kc_ref), (1, vw_ref, vc_ref)):
        base = N_Q_HEADS * HEAD_DIM + which * N_KV_HEADS * HEAD_DIM
        for col in range(N_KV_HEADS // 2):
            c0 = base + col * V7X_LANES
            x = qkv[:, c0:c0 + V7X_LANES]
            xs = pltpu.roll(x, half, axis=1)
            zero = jnp.zeros_like(x)
            variants = {(2 * col, 0): jnp.where(low, x, zero),
                        (2 * col, 1): jnp.where(low, zero, xs),
                        (2 * col + 1, 0): jnp.where(low, xs, zero),
                        (2 * col + 1, 1): jnp.where(low, zero, x)}
            for (kh, var), val in variants.items():
                cur = val.astype(BF16)
                prv = jnp.where(block_start, zero, val).astype(BF16)
                win_ref[kh, var, 0, 0:BLOCK, 0:V7X_LANES] = carry_ref[kh, var]
                for j in range(nblk):
                    blk = slice(j * BLOCK, (j + 1) * BLOCK)
                    win_ref[kh, var, j, BLOCK:2 * BLOCK, 0:V7X_LANES] = cur[blk]
                    if j + 1 < nblk:
                        win_ref[kh, var, j + 1, 0:BLOCK, 0:V7X_LANES] = prv[blk]
                carry_ref[kh, var] = prv[(nblk - 1) * BLOCK:]

    lane2 = lax.broadcasted_iota(jnp.int32, (2 * BLOCK, V7X_LANES), 1)

    def block_body(j, carry):
        r0 = pl.multiple_of(j * BLOCK, BLOCK)
        first = jnp.logical_and(seq_start, j == 0).astype(jnp.int32)
        for kh in range(N_KV_HEADS):
            q2 = q_ref[kh, j]
            ls = []
            acc = None
            for eo in range(2):
                kb = kw_ref[kh, eo, j]
                vb = vw_ref[kh, eo, j]
                s = lax.dot_general(q2, kb, (((1,), (1,)), ((), ())),
                                    preferred_element_type=F32)
                h_top = 4 * kh + eo
                bias = jnp.concatenate([bias_ref[h_top], bias_ref[h_top + 2]], axis=0)
                s = jnp.minimum(s + bias, cap_ref[first])
                m = jnp.max(s, axis=-1, keepdims=True)
                p = jnp.exp2(s - m).astype(BF16)
                pv = jnp.dot(p, vb, preferred_element_type=F32)
                ls.append(pv[:, V7X_LANES:])
                acc = pv[:, :V7X_LANES] if acc is None else acc + pv[:, :V7X_LANES]
            out = acc / jnp.where(lane2 < half, ls[0], ls[1])
            out = out.astype(BF16)
            c0 = kh * 2 * V7X_LANES
            ao_ref[pl.ds(r0, BLOCK), c0:c0 + V7X_LANES] = out[:BLOCK]
            ao_ref[pl.ds(r0, BLOCK), c0 + V7X_LANES:c0 + 2 * V7X_LANES] = out[BLOCK:]
        return carry

    lax.fori_loop(0, nblk, block_body, 0, unroll=True)

    f = jnp.dot(ao_ref[...], wo_ref[...], preferred_element_type=F32) + bo_ref[...]
    o_ref[...] = h + _rms(f, post_ref[...])


def _t5_causal_bucket_np(dist):
    dist = np.maximum(dist, 0)
    max_exact = NUM_BUCKETS // 2
    large = max_exact + (
        np.log(np.maximum(dist, 1).astype(np.float32) / np.float32(max_exact))
        / np.float32(math.log(MAX_DISTANCE / max_exact)) * np.float32(NUM_BUCKETS - max_exact)
    ).astype(np.int32)
    large = np.minimum(large, NUM_BUCKETS - 1)
    return np.where(dist < max_exact, dist, large)


def _attn_tables():
    q_loc = np.arange(BLOCK, dtype=np.int32)[:, None] + BLOCK
    s_loc = np.arange(2 * BLOCK, dtype=np.int32)[None, :]
    dist = q_loc - s_loc
    band = (dist >= 0) & (dist < WINDOW)
    bucket = _t5_causal_bucket_np(dist).astype(np.int32)
    first = band & (s_loc >= BLOCK)
    valid = np.stack([np.tile(band, (2, 1)), np.tile(first, (2, 1))])
    assert not valid[:, :, 0].any()
    valid[:, :, 0] = True
    cap = np.where(valid, np.inf, -np.inf).astype(np.float32)
    return jnp.asarray(bucket), jnp.asarray(cap)


def _attn_call(h, seq_len, pre_g, post_g, w_qkv, b_qkv, w_o, b_o, sinks, rel_bias):
    t, d = h.shape
    tm = TOKEN_TILE
    assert seq_len % tm == 0 and tm % BLOCK == 0
    nblk = tm // BLOCK
    bucket, cap = _attn_tables()
    return pl.pallas_call(
        functools.partial(_attn_kernel, seq_len // tm),
        grid=(t // tm,),
        in_specs=[
            _tile_spec(tm, d),
            _const_spec((1, d)),
            _const_spec((1, d)),
            _const_spec(w_qkv.shape),
            _const_spec((1, w_qkv.shape[1])),
            _const_spec(w_o.shape),
            _const_spec((1, d)),
            _const_spec(bucket.shape),
            _const_spec(cap.shape),
            pl.BlockSpec(memory_space=pltpu.SMEM),
            pl.BlockSpec(memory_space=pltpu.SMEM),
        ],
        out_specs=_tile_spec(tm, d),
        out_shape=jax.ShapeDtypeStruct((t, d), F32),
        scratch_shapes=[
            pltpu.VMEM((N_KV_HEADS, nblk, 2 * BLOCK, V7X_LANES), BF16),
            pltpu.VMEM((N_KV_HEADS, 2, nblk, 2 * BLOCK, V7X_LANES), BF16),
            pltpu.VMEM((N_KV_HEADS, 2, nblk, 2 * BLOCK, 2 * V7X_LANES), BF16),
            pltpu.VMEM((N_KV_HEADS, 2, BLOCK, V7X_LANES), BF16),
            pltpu.VMEM((N_KV_HEADS, 2, BLOCK, V7X_LANES), BF16),
            pltpu.VMEM((tm, d), BF16),
            pltpu.VMEM((N_Q_HEADS, BLOCK, 2 * BLOCK), F32),
        ],
        compiler_params=_params(),
        name="attn_mixer",
    )(h, pre_g, post_g, w_qkv, b_qkv, w_o, b_o, bucket, cap, sinks, rel_bias)


def kernel(x, mix_pre_g, mix_post_g, ffn_pre_g, ffn_post_g, conv_w_in, conv_b_in, conv_dw_w, conv_dw_b, conv_ln_g, conv_ln_b, conv_w_out, conv_b_out, attn_w_qkv, attn_b_qkv, attn_w_o, attn_b_o, attn_sinks, rel_bias, ffn_w_gate_up, ffn_w_down):
    b, s, d = x.shape
    depth = mix_pre_g.shape[0]
    h = x.reshape(b * s, d)
    row = lambda v: v.reshape(1, -1)
    w_gu = ffn_w_gate_up.astype(BF16)
    w_d = ffn_w_down.astype(BF16)
    for i in range(depth):
        j = i // 2
        if i % 2 == 0:
            h = _conv_call(
                h, s, row(mix_pre_g[i]), row(mix_post_g[i]),
                conv_w_in[j].astype(BF16), row(conv_b_in[j]), conv_dw_w[j],
                row(conv_dw_b[j]), row(conv_ln_g[j]), row(conv_ln_b[j]),
                conv_w_out[j].astype(BF16), row(conv_b_out[j]))
        else:
            h = _attn_call(
                h, s, row(mix_pre_g[i]), row(mix_post_g[i]),
                attn_w_qkv[j].astype(BF16), row(attn_b_qkv[j]),
                attn_w_o[j].astype(BF16), row(attn_b_o[j]), attn_sinks[j], rel_bias)
        h = _ffn_call(h, i, row(ffn_pre_g[i]), row(ffn_post_g[i]), w_gu, w_d)
    return h.reshape(b, s, d)
```

```python
import functools
import math

import jax
import jax.numpy as jnp
import numpy as np
from jax import lax
from jax.experimental import pallas as pl
from jax.experimental.pallas import tpu as pltpu

D_MODEL = 1024
CONV_WIDTH = 31
HEAD_DIM = 64
N_Q_HEADS = D_MODEL // HEAD_DIM
N_KV_HEADS = 4
WINDOW = 128
BLOCK = 128
NUM_BUCKETS = 32
MAX_DISTANCE = 128
D_FF = ((8 * D_MODEL // 3 + 255) // 256) * 256
EPS = 1e-6
LOG2_E = math.log2(math.e)

V7X_LANES = 128
V7X_SUBLANES = 8
V7X_VMEM_BYTES = 64 * 1024 * 1024

TOKEN_TILE = 1024
FFN_TILE = 1024
FFN_ROWS = 256
CONV_HALO = 32
CONV_SUBTILE = 64
VMEM_LIMIT = V7X_VMEM_BYTES * 7 // 8

F32 = jnp.float32
BF16 = jnp.bfloat16


def _rms(x, g):
    ms = jnp.mean(x * x, axis=-1, keepdims=True)
    return x * lax.rsqrt(ms + EPS) * g


def _sigmoid(x):
    return 1.0 / (1.0 + jnp.exp2(x * -LOG2_E))


def _const_spec(shape):
    zeros = (0,) * len(shape)
    return pl.BlockSpec(shape, lambda i: zeros, pipeline_mode=pl.Buffered(1))


def _layer_spec(stacked, layer):
    zeros = (0,) * (stacked.ndim - 1)
    return pl.BlockSpec((None,) + stacked.shape[1:], lambda i: (layer,) + zeros,
                        pipeline_mode=pl.Buffered(1))


def _tile_spec(tm, d):
    return pl.BlockSpec((tm, d), lambda i: (i, 0))


def _params():
    return pltpu.CompilerParams(
        dimension_semantics=("arbitrary",), vmem_limit_bytes=VMEM_LIMIT)


def _ffn_kernel(h_ref, pre_ref, post_ref, wgu_ref, wd_ref, o_ref):
    tm = h_ref.shape[0]
    for r0 in range(0, tm, FFN_ROWS):
        rows = slice(r0, r0 + FFN_ROWS)
        h = h_ref[rows, :]
        xn = _rms(h, pre_ref[...]).astype(BF16)
        gu = jnp.dot(xn, wgu_ref[...], preferred_element_type=F32)
        g, u = gu[:, :D_FF], gu[:, D_FF:]
        a = (g * _sigmoid(g) * u).astype(BF16)
        f = jnp.dot(a, wd_ref[...], preferred_element_type=F32)
        o_ref[rows, :] = h + _rms(f, post_ref[...])


def _ffn_call(h, layer, pre_g, post_g, w_gu, w_d):
    t, d = h.shape
    tm = FFN_TILE
    assert t % tm == 0 and tm % FFN_ROWS == 0
    return pl.pallas_call(
        _ffn_kernel,
        grid=(t // tm,),
        in_specs=[
            _tile_spec(tm, d),
            _const_spec((1, d)),
            _const_spec((1, d)),
            _layer_spec(w_gu, layer),
            _layer_spec(w_d, layer),
        ],
        out_specs=_tile_spec(tm, d),
        out_shape=jax.ShapeDtypeStruct((t, d), F32),
        compiler_params=_params(),
        name="ffn",
    )(h, pre_g, post_g, w_gu, w_d)


def _conv_kernel(tiles_per_seq, h_ref, pre_ref, post_ref, win_ref, bin_ref,
                 dww_ref, dwb_ref, lng_ref, lnb_ref, wout_ref, bout_ref,
                 o_ref, ext_ref, y_ref):
    tm, d = h_ref.shape
    n_lane = d // V7X_LANES
    i = pl.program_id(0)

    @pl.when(i % tiles_per_seq == 0)
    def _():
        ext_ref[:, 0:CONV_HALO, :] = jnp.zeros((n_lane, CONV_HALO, V7X_LANES), F32)

    h = h_ref[...]
    xn = _rms(h, pre_ref[...]).astype(BF16)
    off0 = CONV_HALO - (CONV_WIDTH - 1)
    ts = CONV_SUBTILE

    def chunk_body(cc, carry):
        for r0 in range(0, tm, ts):
            acc = jnp.broadcast_to(dwb_ref[cc], (ts, V7X_LANES))
            for k in range(CONV_WIDTH):
                acc = acc + ext_ref[cc, r0 + off0 + k:r0 + off0 + k + ts, :] * dww_ref[cc, k:k + 1, :]
            y_ref[cc, r0:r0 + ts, :] = acc
        return carry

    wide = 2 * V7X_LANES
    for c0 in range(0, d, wide):
        val = jnp.dot(xn, win_ref[:, c0:c0 + wide], preferred_element_type=F32) + bin_ref[:, c0:c0 + wide]
        gate = jnp.dot(xn, win_ref[:, d + c0:d + c0 + wide],
                       preferred_element_type=F32) + bin_ref[:, d + c0:d + c0 + wide]
        glu = val * _sigmoid(gate)
        for cc in range(c0 // V7X_LANES, (c0 + wide) // V7X_LANES):
            lo = cc * V7X_LANES - c0
            ext_ref[cc, CONV_HALO:CONV_HALO + tm, :] = glu[:, lo:lo + V7X_LANES]
            chunk_body(cc, 0)

    ext_ref[:, 0:CONV_HALO, :] = ext_ref[:, tm:tm + CONV_HALO, :]

    y = jnp.concatenate([y_ref[cc] for cc in range(n_lane)], axis=1)
    mu = jnp.mean(y, axis=-1, keepdims=True)
    yc = y - mu
    var = jnp.mean(yc * yc, axis=-1, keepdims=True)
    z = yc * lax.rsqrt(var + EPS) * lng_ref[...] + lnb_ref[...]
    z = (z * _sigmoid(z)).astype(BF16)
    f = jnp.dot(z, wout_ref[...], preferred_element_type=F32) + bout_ref[...]
    o_ref[...] = h + _rms(f, post_ref[...])


def _conv_call(h, seq_len, pre_g, post_g, w_in, b_in, dw_w, dw_b, ln_g, ln_b,
               w_out, b_out):
    t, d = h.shape
    tm = TOKEN_TILE
    assert seq_len % tm == 0 and CONV_HALO >= CONV_WIDTH - 1
    n_lane = d // V7X_LANES
    dw_w = dw_w.reshape(CONV_WIDTH, n_lane, V7X_LANES).transpose(1, 0, 2)
    dw_b = dw_b.reshape(n_lane, 1, V7X_LANES)
    return pl.pallas_call(
        functools.partial(_conv_kernel, seq_len // tm),
        grid=(t // tm,),
        in_specs=[
            _tile_spec(tm, d),
            _const_spec((1, d)),
            _const_spec((1, d)),
            _const_spec(w_in.shape),
            _const_spec((1, 2 * d)),
            _const_spec(dw_w.shape),
            _const_spec(dw_b.shape),
            _const_spec((1, d)),
            _const_spec((1, d)),
            _const_spec(w_out.shape),
            _const_spec((1, d)),
        ],
        out_specs=_tile_spec(tm, d),
        out_shape=jax.ShapeDtypeStruct((t, d), F32),
        scratch_shapes=[
            pltpu.VMEM((n_lane, tm + CONV_HALO, V7X_LANES), F32),
            pltpu.VMEM((n_lane, tm, V7X_LANES), F32),
        ],
        compiler_params=_params(),
        name="conv_mixer",
    )(h, pre_g, post_g, w_in, b_in, dw_w, dw_b, ln_g, ln_b, w_out, b_out)


def _attn_kernel(tiles_per_seq, h_ref, pre_ref, post_ref, wqkv_ref, bqkv_ref,
                 wo_ref, bo_ref, bucket_ref, cap_ref, sink_ref, relb_ref,
                 o_ref, q_ref, kw_ref, vw_ref, kc_ref, vc_ref, ao_ref, bias_ref):
    tm, d = h_ref.shape
    nblk = tm // BLOCK
    half = V7X_LANES // 2
    i = pl.program_id(0)
    seq_start = i % tiles_per_seq == 0

    @pl.when(i == 0)
    def _():
        vw_ref[:, :, :, :, V7X_LANES:] = jnp.ones(
            (N_KV_HEADS, 2, nblk, 2 * BLOCK, V7X_LANES), BF16)
        bucket = bucket_ref[...]
        key_col = lax.broadcasted_iota(jnp.int32, bucket.shape, 1)

        def head_body(hh, carry):
            tile = jnp.zeros(bucket.shape, F32)
            for b in range(NUM_BUCKETS):
                tile = jnp.where(bucket == b, relb_ref[b, hh], tile)
            bias_ref[hh] = jnp.where(key_col == 0, sink_ref[hh], tile) * LOG2_E
            return carry

        lax.fori_loop(0, N_Q_HEADS, head_body, 0)

    @pl.when(seq_start)
    def _():
        z = jnp.zeros((N_KV_HEADS, 2, BLOCK, V7X_LANES), BF16)
        kc_ref[...] = z
        vc_ref[...] = z

    h = h_ref[...]
    xn = _rms(h, pre_ref[...]).astype(BF16)

    scale = HEAD_DIM ** -0.5 * LOG2_E
    qkv = jnp.dot(xn, wqkv_ref[...], preferred_element_type=F32) + bqkv_ref[...]
    for kh in range(N_KV_HEADS):
        for pr in range(2):
            c0 = (2 * kh + pr) * V7X_LANES
            qp = (qkv[:, c0:c0 + V7X_LANES] * scale).astype(BF16)
            for j in range(nblk):
                q_ref[kh, j, pr * BLOCK:(pr + 1) * BLOCK, :] = qp[j * BLOCK:(j + 1) * BLOCK, :]

    lane = lax.broadcasted_iota(jnp.int32, (tm, V7X_LANES), 1)
    low = lane < half
    row = lax.broadcasted_iota(jnp.int32, (tm, V7X_LANES), 0)
    block_start = row % BLOCK == 0
    for which, win_ref, carry_ref in ((0, kw_ref, kc_ref), (1, vw_ref, vc_ref)):
        base = N_Q_HEADS * HEAD_DIM + which * N_KV_HEADS * HEAD_DIM
        for col in range(N_KV_HEADS // 2):
            c0 = base + col * V7X_LANES
            x = qkv[:, c0:c0 + V7X_LANES]
            xs = pltpu.roll(x, half, axis=1)
            zero = jnp.zeros_like(x)
            variants = {(2 * col, 0): jnp.where(low, x, zero),
                        (2 * col, 1): jnp.where(low, zero, xs),
                        (2 * col + 1, 0): jnp.where(low, xs, zero),
                        (2 * col + 1, 1): jnp.where(low, zero, x)}
            for (kh, var), val in variants.items():
                cur = val.astype(BF16)
                prv = jnp.where(block_start, zero, val).astype(BF16)
                win_ref[kh, var, 0, 0:BLOCK, 0:V7X_LANES] = carry_ref[kh, var]
                for j in range(nblk):
                    blk = slice(j * BLOCK, (j + 1) * BLOCK)
                    win_ref[kh, var, j, BLOCK:2 * BLOCK, 0:V7X_LANES] = cur[blk]
                    if j + 1 < nblk:
                        win_ref[kh, var, j + 1, 0:BLOCK, 0:V7X_LANES] = prv[blk]
                carry_ref[kh, var] = prv[(nblk - 1) * BLOCK:]

    lane2 = lax.broadcasted_iota(jnp.int32, (2 * BLOCK, V7X_LANES), 1)

    def block_body(j, carry):
        r0 = pl.multiple_of(j * BLOCK, BLOCK)
        first = jnp.logical_and(seq_start, j == 0).astype(jnp.int32)
        for kh in range(N_KV_HEADS):
            q2 = q_ref[kh, j]
            ls = []
            acc = None
            for eo in range(2):
                kb = kw_ref[kh, eo, j]
                vb = vw_ref[kh, eo, j]
                s = lax.dot_general(q2, kb, (((1,), (1,)), ((), ())),
                                    preferred_element_type=F32)
                h_top = 4 * kh + eo
                bias = jnp.concatenate([bias_ref[h_top], bias_ref[h_top + 2]], axis=0)
                s = jnp.minimum(s + bias, cap_ref[first])
                m = jnp.max(s, axis=-1, keepdims=True)
                p = jnp.exp2(s - m).astype(BF16)
                pv = jnp.dot(p, vb, preferred_element_type=F32)
                ls.append(pv[:, V7X_LANES:])
                acc = pv[:, :V7X_LANES] if acc is None else acc + pv[:, :V7X_LANES]
            out = acc / jnp.where(lane2 < half, ls[0], ls[1])
            out = out.astype(BF16)
            c0 = kh * 2 * V7X_LANES
            ao_ref[pl.ds(r0, BLOCK), c0:c0 + V7X_LANES] = out[:BLOCK]
            ao_ref[pl.ds(r0, BLOCK), c0 + V7X_LANES:c0 + 2 * V7X_LANES] = out[BLOCK:]
        return carry

    lax.fori_loop(0, nblk, block_body, 0, unroll=True)

    f = jnp.dot(ao_ref[...], wo_ref[...], preferred_element_type=F32) + bo_ref[...]
    o_ref[...] = h + _rms(f, post_ref[...])


def _t5_causal_bucket_np(dist):
    dist = np.maximum(dist, 0)
    max_exact = NUM_BUCKETS // 2
    large = max_exact + (
        np.log(np.maximum(dist, 1).astype(np.float32) / np.float32(max_exact))
        / np.float32(math.log(MAX_DISTANCE / max_exact)) * np.float32(NUM_BUCKETS - max_exact)
    ).astype(np.int32)
    large = np.minimum(large, NUM_BUCKETS - 1)
    return np.where(dist < max_exact, dist, large)


def _attn_tables():
    q_loc = np.arange(BLOCK, dtype=np.int32)[:, None] + BLOCK
    s_loc = np.arange(2 * BLOCK, dtype=np.int32)[None, :]
    dist = q_loc - s_loc
    band = (dist >= 0) & (dist < WINDOW)
    bucket = _t5_causal_bucket_np(dist).astype(np.int32)
    first = band & (s_loc >= BLOCK)
    valid = np.stack([np.tile(band, (2, 1)), np.tile(first, (2, 1))])
    assert not valid[:, :, 0].any()
    valid[:, :, 0] = True
    cap = np.where(valid, np.inf, -np.inf).astype(np.float32)
    return jnp.asarray(bucket), jnp.asarray(cap)


def _attn_call(h, seq_len, pre_g, post_g, w_qkv, b_qkv, w_o, b_o, sinks, rel_bias):
    t, d = h.shape
    tm = TOKEN_TILE
    assert seq_len % tm == 0 and tm % BLOCK == 0
    nblk = tm // BLOCK
    bucket, cap = _attn_tables()
    return pl.pallas_call(
        functools.partial(_attn_kernel, seq_len // tm),
        grid=(t // tm,),
        in_specs=[
            _tile_spec(tm, d),
            _const_spec((1, d)),
            _const_spec((1, d)),
            _const_spec(w_qkv.shape),
            _const_spec((1, w_qkv.shape[1])),
            _const_spec(w_o.shape),
            _const_spec((1, d)),
            _const_spec(bucket.shape),
            _const_spec(cap.shape),
            pl.BlockSpec(memory_space=pltpu.SMEM),
            pl.BlockSpec(memory_space=pltpu.SMEM),
        ],
        out_specs=_tile_spec(tm, d),
        out_shape=jax.ShapeDtypeStruct((t, d), F32),
        scratch_shapes=[
            pltpu.VMEM((N_KV_HEADS, nblk, 2 * BLOCK, V7X_LANES), BF16),
            pltpu.VMEM((N_KV_HEADS, 2, nblk, 2 * BLOCK, V7X_LANES), BF16),
            pltpu.VMEM((N_KV_HEADS, 2, nblk, 2 * BLOCK, 2 * V7X_LANES), BF16),
            pltpu.VMEM((N_KV_HEADS, 2, BLOCK, V7X_LANES), BF16),
            pltpu.VMEM((N_KV_HEADS, 2, BLOCK, V7X_LANES), BF16),
            pltpu.VMEM((tm, d), BF16),
            pltpu.VMEM((N_Q_HEADS, BLOCK, 2 * BLOCK), F32),
        ],
        compiler_params=_params(),
        name="attn_mixer",
    )(h, pre_g, post_g, w_qkv, b_qkv, w_o, b_o, bucket, cap, sinks, rel_bias)


def kernel(x, mix_pre_g, mix_post_g, ffn_pre_g, ffn_post_g, conv_w_in, conv_b_in, conv_dw_w, conv_dw_b, conv_ln_g, conv_ln_b, conv_w_out, conv_b_out, attn_w_qkv, attn_b_qkv, attn_w_o, attn_b_o, attn_sinks, rel_bias, ffn_w_gate_up, ffn_w_down):
    b, s, d = x.shape
    depth = mix_pre_g.shape[0]
    h = x.reshape(b * s, d)
    row = lambda v: v.reshape(1, -1)
    w_gu = ffn_w_gate_up.astype(BF16)
    w_d = ffn_w_down.astype(BF16)
    for i in range(depth):
        j = i // 2
        if i % 2 == 0:
            h = _conv_call(
                h, s, row(mix_pre_g[i]), row(mix_post_g[i]),
                conv_w_in[j].astype(BF16), row(conv_b_in[j]), conv_dw_w[j],
                row(conv_dw_b[j]), row(conv_ln_g[j]), row(conv_ln_b[j]),
                conv_w_out[j].astype(BF16), row(conv_b_out[j]))
        else:
            h = _attn_call(
                h, s, row(mix_pre_g[i]), row(mix_post_g[i]),
                attn_w_qkv[j].astype(BF16), row(attn_b_qkv[j]),
                attn_w_o[j].astype(BF16), row(attn_b_o[j]), attn_sinks[j], rel_bias)
        h = _ffn_call(h, i, row(ffn_pre_g[i]), row(ffn_post_g[i]), w_gu, w_d)
    return h.reshape(b, s, d)
```
